```python
import jax, jax.numpy as jnp
from jax import lax
import numpy as np

D_MODEL = 1024
BATCH = 32
SEQ = 2048
DEPTH = 4
DEC_BATCH = 32
DEC_SEQ = 64
PAST_LEN = 4096

CHUNK = 64
D_MIX = D_MODEL
D_HGRN = D_MIX // 2
N_HGRN_HEADS = 4
HGRN_HEAD_DIM = D_HGRN // N_HGRN_HEADS
D_SGU = D_MIX - D_HGRN
N_SGU_HEADS = 4
SGU_HEAD_DIM = D_SGU // N_SGU_HEADS
SGU_CHUNK = 128
D_FF = 2816
FFN_RESIDUAL_WEIGHT = 0.5
D_IN = 4 * D_HGRN + 2 * D_SGU
EPS = 1e-6

kernel_name = "hybrid_hgrn2_gmlp_macaron_stream_step"


def rms_norm(x, g):
    x32 = x.astype(jnp.float32)
    y = x32 * lax.rsqrt(jnp.mean(x32 * x32, axis=-1, keepdims=True) + EPS)
    return (y * g.astype(jnp.float32)).astype(x.dtype)


def swiglu(x, w1, w3, w2):
    return (jax.nn.silu(x @ w1) * (x @ w3)) @ w2


def layer_lower_bounds(lb_logits):
    p = jax.nn.softmax(lb_logits.astype(jnp.float32), axis=0)
    return jnp.cumsum(p, axis=0) - p[0:1]


def gla_block_step(S, blk):
    q, k, v, logf = blk
    c = q.shape[2]
    b = jnp.cumsum(logf, axis=2)
    o_inter = jnp.einsum('bhtk,bhkv->bhtv', q * jnp.exp(b), S)
    mask = jnp.tril(jnp.ones((c, c), dtype=bool))[:, :, None]
    diff = b[:, :, :, None, :] - b[:, :, None, :, :]
    decay = jnp.exp(jnp.where(mask, diff, -jnp.inf))
    att = jnp.einsum('bhtk,bhtsk,bhsk->bhts', q, decay, k)
    o = o_inter + jnp.einsum('bhts,bhsv->bhtv', att, v)
    b_last = b[:, :, -1, :]
    S_new = jnp.exp(b_last)[..., None] * S + jnp.einsum(
        'bhsk,bhsv->bhkv', k * jnp.exp(b_last[:, :, None, :] - b), v)
    return S_new, o


def hgrn2_mixer(zq, zf, zi, zg, lb, out_gain, S0):
    B, L, _ = zq.shape
    H, Dh = N_HGRN_HEADS, HGRN_HEAD_DIM
    heads = lambda t: t.astype(jnp.float32).reshape(B, L, H, Dh)
    q = jax.nn.silu(heads(zq)) * (Dh ** -0.5)
    f = lb.reshape(H, Dh) + (1.0 - lb.reshape(H, Dh)) * jax.nn.sigmoid(heads(zf))
    logf = jnp.log(f)
    k = 1.0 - f
    v = heads(zi)
    c = min(L, CHUNK)
    n = L // c
    to_blocks = lambda t: t.reshape(B, n, c, H, Dh).transpose(1, 0, 3, 2, 4)
    S_fin, o = lax.scan(gla_block_step, S0, (to_blocks(q), to_blocks(k), to_blocks(v), to_blocks(logf)))
    o = o.transpose(1, 0, 3, 2, 4).reshape(B, L, H, Dh)
    o = o * lax.rsqrt(jnp.mean(o * o, axis=-1, keepdims=True) + EPS) * out_gain.astype(jnp.float32).reshape(H, Dh)
    o = o * jax.nn.silu(heads(zg))
    return o.reshape(B, L, D_HGRN).astype(zq.dtype), S_fin


def sgu_mixer(zu, zv, ln_g, ln_b, w_s, b_s):
    B, L, _ = zu.shape
    Hs, Ch = N_SGU_HEADS, SGU_HEAD_DIM
    c = min(L, SGU_CHUNK)
    n = L // c
    u = jax.nn.gelu(zu).reshape(B, L, Hs, Ch)
    v = jax.nn.gelu(zv).astype(jnp.float32).reshape(B, L, Hs, Ch)
    mu = jnp.mean(v, axis=-1, keepdims=True)
    var = jnp.mean(jnp.square(v - mu), axis=-1, keepdims=True)
    vn = (v - mu) * lax.rsqrt(var + EPS) * ln_g.astype(jnp.float32).reshape(Hs, Ch) \
        + ln_b.astype(jnp.float32).reshape(Hs, Ch)
    vn = vn.astype(zu.dtype)
    ws = jnp.tril(w_s[:, :c, :c])
    mixed = jnp.einsum('hts,bnshd->bnthd', ws, vn.reshape(B, n, c, Hs, Ch)) \
        + b_s[:, :c].T[None, None, :, :, None]
    out = u * mixed.reshape(B, L, Hs, Ch)
    return out.reshape(B, L, D_SGU), vn.reshape(B, L, D_SGU)


def trunk(x, S0_stack, lbs, norm_ffn1, ffn1_w1, ffn1_w3, ffn1_w2, norm_mix, w_in, hgrn_out_norm,
          sgu_ln_g, sgu_ln_b, sgu_w_s, sgu_b_s, w_out, norm_ffn2, ffn2_w1, ffn2_w3, ffn2_w2, final_norm):
    states, vrows = [], []
    cuts = [D_HGRN, 2 * D_HGRN, 3 * D_HGRN, 4 * D_HGRN, 4 * D_HGRN + D_SGU]
    for l in range(DEPTH):
        x = x + FFN_RESIDUAL_WEIGHT * swiglu(rms_norm(x, norm_ffn1[l]), ffn1_w1[l], ffn1_w3[l], ffn1_w2[l])
        h = rms_norm(x, norm_mix[l])
        zq, zf, zi, zg, zu, zv = jnp.split(h @ w_in[l], cuts, axis=-1)
        o_a, S = hgrn2_mixer(zq, zf, zi, zg, lbs[l], hgrn_out_norm[l], S0_stack[l].astype(jnp.float32))
        o_b, vn = sgu_mixer(zu, zv, sgu_ln_g[l], sgu_ln_b[l], sgu_w_s[l], sgu_b_s[l])
        x = x + jnp.concatenate([o_a, o_b.astype(o_a.dtype)], axis=-1) @ w_out[l]
        x = x + FFN_RESIDUAL_WEIGHT * swiglu(rms_norm(x, norm_ffn2[l]), ffn2_w1[l], ffn2_w3[l], ffn2_w2[l])
        states.append(S)
        vrows.append(vn)
    return rms_norm(x, final_norm), jnp.stack(states), jnp.stack(vrows)


def setup_inputs(seed: int = 0) -> dict:
    key = jax.random.key(seed)
    ks = jax.random.split(key, 24)
    f32 = jnp.float32
    nrm = lambda k, shape, s: jax.random.normal(k, shape, f32) * s
    gain = lambda k, shape: 1.0 + 0.02 * jax.random.normal(k, shape, f32)
    return {
        "x_prompt": nrm(ks[0], (BATCH, SEQ, D_MODEL), 1.0),
        "x_sample": nrm(ks[1], (DEC_BATCH, DEC_SEQ, D_MODEL), 1.0),
        "state_hgrn": nrm(ks[2], (DEPTH, DEC_BATCH, N_HGRN_HEADS, HGRN_HEAD_DIM, HGRN_HEAD_DIM), 0.5),
        "lb_logits": nrm(ks[3], (DEPTH, D_HGRN), 1.0),
        "norm_ffn1": gain(ks[4], (DEPTH, D_MODEL)),
        "ffn1_w1": nrm(ks[5], (DEPTH, D_MODEL, D_FF), D_MODEL ** -0.5),
        "ffn1_w3": nrm(ks[6], (DEPTH, D_MODEL, D_FF), D_MODEL ** -0.5),
        "ffn1_w2": nrm(ks[7], (DEPTH, D_FF, D_MODEL), D_FF ** -0.5),
        "norm_mix": gain(ks[8], (DEPTH, D_MODEL)),
        "w_in": nrm(ks[9], (DEPTH, D_MODEL, D_IN), D_MODEL ** -0.5),
        "hgrn_out_norm": gain(ks[10], (DEPTH, D_HGRN)),
        "sgu_ln_g": gain(ks[11], (DEPTH, D_SGU)),
        "sgu_ln_b": nrm(ks[12], (DEPTH, D_SGU), 0.02),
        "sgu_w_s": nrm(ks[13], (DEPTH, N_SGU_HEADS, SGU_CHUNK, SGU_CHUNK), SGU_CHUNK ** -0.5),
        "sgu_b_s": gain(ks[14], (DEPTH, N_SGU_HEADS, SGU_CHUNK)),
        "w_out": nrm(ks[15], (DEPTH, D_MIX, D_MODEL), D_MIX ** -0.5),
        "norm_ffn2": gain(ks[16], (DEPTH, D_MODEL)),
        "ffn2_w1": nrm(ks[17], (DEPTH, D_MODEL, D_FF), D_MODEL ** -0.5),
        "ffn2_w3": nrm(ks[18], (DEPTH, D_MODEL, D_FF), D_MODEL ** -0.5),
        "ffn2_w2": nrm(ks[19], (DEPTH, D_FF, D_MODEL), D_FF ** -0.5),
        "final_norm": gain(ks[20], (D_MODEL,)),
    }


def reference(x_prompt, x_sample, state_hgrn, lb_logits, norm_ffn1, ffn1_w1, ffn1_w3, ffn1_w2, norm_mix,
              w_in, hgrn_out_norm, sgu_ln_g, sgu_ln_b, sgu_w_s, sgu_b_s, w_out, norm_ffn2, ffn2_w1,
              ffn2_w3, ffn2_w2, final_norm):
    lbs = layer_lower_bounds(lb_logits)
    weights = (norm_ffn1, ffn1_w1, ffn1_w3, ffn1_w2, norm_mix, w_in, hgrn_out_norm, sgu_ln_g, sgu_ln_b,
               sgu_w_s, sgu_b_s, w_out, norm_ffn2, ffn2_w1, ffn2_w3, ffn2_w2, final_norm)
    zeros = jnp.zeros((DEPTH, x_prompt.shape[0], N_HGRN_HEADS, HGRN_HEAD_DIM, HGRN_HEAD_DIM), jnp.float32)
    y_prompt, s_prompt, _ = trunk(x_prompt, zeros, lbs, *weights)
    y_sample, s_sample, v_sample = trunk(x_sample, state_hgrn, lbs, *weights)
    state_hgrn_prompt = s_prompt.astype(x_prompt.dtype)
    state_hgrn_sample = s_sample.astype(state_hgrn.dtype)
    state_sgu_v_sample = v_sample.astype(x_sample.dtype)
    return (y_prompt, y_sample, state_hgrn_prompt, state_hgrn_sample, state_sgu_v_sample)
```

```python
import functools

import jax
import jax.numpy as jnp
from jax import lax
from jax.experimental import pallas as pl
from jax.experimental.pallas import tpu as pltpu

F32 = jnp.float32
BF16 = jnp.bfloat16

D_MODEL = 1024
D_FF = 2816
D_HGRN = 512
D_SGU = 512
N_HEADS = 4
HEAD = 128
D_IN = 4 * D_HGRN + 2 * D_SGU
BLOCK = 64
SUB = 8
SGU_CHUNK = 128
EPS = 1e-6
FFN_HALF = 0.5

TOKEN_TILE = 512
FF_CHUNK = 256
VMEM_LIMIT_BYTES = 56 * 1024 * 1024


def _rms(x, g):
    return x * lax.rsqrt(jnp.mean(x * x, axis=-1, keepdims=True) + EPS) * g


def _dot(a, b):
    return jnp.dot(a, b, preferred_element_type=F32)


def _dot_nt(a, b):
    return lax.dot_general(a, b, (((1,), (1,)), ((), ())), preferred_element_type=F32)


def _dot_tn(a, b):
    return lax.dot_general(a, b, (((0,), (0,)), ((), ())), preferred_element_type=F32)


def _ffn_kernel(*refs, final):
    if final:
        x_ref, g_ref, w1_ref, w3_ref, w2_ref, fg_ref, o_ref, h_ref = refs
    else:
        x_ref, g_ref, w1_ref, w3_ref, w2_ref, o_ref, h_ref = refs
    x = x_ref[...]
    xn = _rms(x, g_ref[...]).astype(BF16)
    for c in range(D_FF // FF_CHUNK):
        cs = slice(c * FF_CHUNK, (c + 1) * FF_CHUNK)
        a = _dot(xn, w1_ref[:, cs])
        b = _dot(xn, w3_ref[:, cs])
        h_ref[:, cs] = (jax.nn.silu(a) * b).astype(BF16)
    out = x + FFN_HALF * _dot(h_ref[...], w2_ref[...])
    if final:
        out = _rms(out, fg_ref[...])
    o_ref[...] = out


def _ffn(x, g, w1, w3, w2, layer, final_g=None):
    t = x.shape[0]
    tm = min(TOKEN_TILE, t)
    assert t % tm == 0
    final = final_g is not None
    const = lambda i: (layer, 0, 0)
    in_specs = [
        pl.BlockSpec((tm, D_MODEL), lambda i: (i, 0)),
        pl.BlockSpec((None, 1, D_MODEL), const),
        pl.BlockSpec((None, D_MODEL, D_FF), const),
        pl.BlockSpec((None, D_MODEL, D_FF), const),
        pl.BlockSpec((None, D_FF, D_MODEL), const),
    ]
    args = [x, g, w1, w3, w2]
    if final:
        in_specs.append(pl.BlockSpec((1, D_MODEL), lambda i: (0, 0)))
        args.append(final_g)
    return pl.pallas_call(
        functools.partial(_ffn_kernel, final=final),
        grid=(t // tm,),
        in_specs=in_specs,
        out_specs=pl.BlockSpec((tm, D_MODEL), lambda i: (i, 0)),
        out_shape=jax.ShapeDtypeStruct((t, D_MODEL), F32),
        scratch_shapes=[pltpu.VMEM((tm, D_FF), BF16)],
        compiler_params=pltpu.CompilerParams(
            dimension_semantics=("arbitrary",), vmem_limit_bytes=VMEM_LIMIT_BYTES),
        name="ffn_final" if final else "ffn",
    )(*args)


def _roll_sub(x, d):
    n, c = x.shape
    return pltpu.roll(x.reshape(n // SUB, SUB, c), d, 1).reshape(n, c)


def _level_operand(q, k, b, half):
    qk, ref = [], []
    for r0 in range(0, BLOCK, half):
        upper = (r0 // half) % 2 == 1
        qk.append((q if upper else k)[r0:r0 + half])
        mid = (r0 // (2 * half)) * 2 * half + half - 1
        ref.append(jnp.broadcast_to(b[mid:mid + 1], (half, HEAD)))
    qk = jnp.concatenate(qk, axis=0)
    ref = jnp.concatenate(ref, axis=0)
    return (qk * jnp.exp(-jnp.abs(b - ref))).astype(BF16)


def _mixer_kernel(*refs, layer, tm, chained, tiles_per_seq, sgu_c):
    if chained:
        (x_ref, g_ref, win_ref, lbl_ref, og_ref, lng_ref, lnb_ref, ws_ref, bs_ref, wout_ref,
         xo_ref, so_ref,
         q_s, k_s, v_s, lf_s, gate_s, u_s, vn_s, mix_s, st_s) = refs
        s0_ref = vno_ref = None
    else:
        (x_ref, g_ref, win_ref, lbl_ref, og_ref, lng_ref, lnb_ref, ws_ref, bs_ref, wout_ref, s0_ref,
         xo_ref, so_ref, vno_ref,
         q_s, k_s, v_s, lf_s, gate_s, u_s, vn_s, mix_s, st_s) = refs

    hn = _rms(x_ref[...], g_ref[...]).astype(BF16)

    lg = lbl_ref[...]
    e = jnp.exp(lg - jnp.max(lg, axis=0, keepdims=True))
    p = e / jnp.sum(e, axis=0, keepdims=True)
    lb = p[0:1]
    for j in range(1, layer + 1):
        lb = lb + p[j:j + 1]
    lb = lb - p[0:1]

    def zcols(j):
        return _dot(hn, win_ref[:, j * D_HGRN:(j + 1) * D_HGRN])

    q_s[...] = jax.nn.silu(zcols(0)) * (HEAD ** -0.5)
    f = lb + (1.0 - lb) * jax.nn.sigmoid(zcols(1))
    lf_s[...] = jnp.log(f)
    k_s[...] = 1.0 - f
    v_s[...] = zcols(2)
    gate_s[...] = jax.nn.silu(zcols(3))
    u_s[...] = jax.nn.gelu(zcols(4))
    vg = jax.nn.gelu(zcols(5))
    for h in range(N_HEADS):
        cs = slice(h * HEAD, (h + 1) * HEAD)
        vh = vg[:, cs]
        mu = jnp.mean(vh, axis=-1, keepdims=True)
        var = jnp.mean(jnp.square(vh - mu), axis=-1, keepdims=True)
        vn = (vh - mu) * lax.rsqrt(var + EPS) * lng_ref[:, cs] + lnb_ref[:, cs]
        if vno_ref is not None:
            vno_ref[:, cs] = vn
        vn_s[:, cs] = vn.astype(BF16)

    ti = lax.broadcasted_iota(jnp.int32, (BLOCK, BLOCK), 0)
    si = lax.broadcasted_iota(jnp.int32, (BLOCK, BLOCK), 1)
    tril = (ti >= si).astype(F32)
    level_masks = []
    half = BLOCK // 2
    while half >= SUB:
        same = (ti // (2 * half)) == (si // (2 * half))
        level_masks.append((half, same & ((ti % (2 * half)) >= half) & ((si % (2 * half)) < half)))
        half //= 2
    row_in_sub = lax.broadcasted_iota(jnp.int32, (BLOCK, HEAD), 0) % SUB
    sub_valid = [row_in_sub >= d for d in range(SUB)]

    if chained:
        @pl.when(pl.program_id(0) % tiles_per_seq == 0)
        def _():
            st_s[...] = jnp.zeros_like(st_s)

    def block_step(i, carry):
        rows = pl.ds(pl.multiple_of(i * BLOCK, BLOCK), BLOCK)
        b_all = jnp.dot(tril, lf_s[rows, :], precision=lax.Precision.HIGHEST,
                        preferred_element_type=F32)
        for h in range(N_HEADS):
            cs = slice(h * HEAD, (h + 1) * HEAD)
            b = b_all[:, cs]
            q = q_s[rows, cs]
            k = k_s[rows, cs]
            v = v_s[rows, cs]
            vb = v.astype(BF16)
            st = st_s[h] if chained else s0_ref[i, h].T
            o = _dot_nt((q * jnp.exp(b)).astype(BF16), st.astype(BF16))
            att = jnp.zeros((BLOCK, BLOCK), F32)
            for half, mask in level_masks:
                w = _level_operand(q, k, b, half)
                att = jnp.where(mask, _dot_nt(w, w), att)
            o = o + _dot(att.astype(BF16), vb)
            o = o + jnp.sum(q * k, axis=-1, keepdims=True) * v
            for d in range(1, SUB):
                ok = sub_valid[d]
                kd = jnp.where(ok, _roll_sub(k, d), 0.0)
                dd = jnp.where(ok, b - _roll_sub(b, d), 0.0)
                a_d = jnp.sum(q * kd * jnp.exp(dd), axis=-1, keepdims=True)
                o = o + a_d * _roll_sub(v, d)
            b_last = b[BLOCK - 1:BLOCK]
            ks = (k * jnp.exp(b_last - b)).astype(BF16)
            st_new = st * jnp.exp(b_last) + _dot_tn(vb, ks)
            if chained:
                st_s[h] = st_new
            else:
                so_ref[i, h] = st_new.T
            o = o * lax.rsqrt(jnp.mean(o * o, axis=-1, keepdims=True) + EPS) * og_ref[:, cs]
            mix_s[rows, cs] = (o * gate_s[rows, cs]).astype(BF16)
        return carry

    lax.fori_loop(0, tm // BLOCK, block_step, 0)
    if chained:
        for h in range(N_HEADS):
            so_ref[0, h] = st_s[h].T

    tc = lax.broadcasted_iota(jnp.int32, (sgu_c, sgu_c), 0)
    sc = lax.broadcasted_iota(jnp.int32, (sgu_c, sgu_c), 1)
    ws = [jnp.where(tc >= sc, ws_ref[h, :sgu_c, :sgu_c], 0.0).astype(BF16) for h in range(N_HEADS)]

    def sgu_step(j, carry):
        rows = pl.ds(pl.multiple_of(j * sgu_c, sgu_c), sgu_c)
        for h in range(N_HEADS):
            cs = slice(h * HEAD, (h + 1) * HEAD)
            mixed = _dot(ws[h], vn_s[rows, cs]) + bs_ref[:sgu_c, cs]
            mix_s[rows, D_HGRN + h * HEAD:D_HGRN + (h + 1) * HEAD] = (u_s[rows, cs] * mixed).astype(BF16)
        return carry

    lax.fori_loop(0, tm // sgu_c, sgu_step, 0)

    xo_ref[...] = x_ref[...] + _dot(mix_s[...], wout_ref[...])


def _mixer(x, seq_len, s0, g, w_in, lb_logits, out_gain, ln_g, ln_b, w_s, bias, w_out, layer):
    t = x.shape[0]
    n_seq = t // seq_len
    tm = min(TOKEN_TILE, t)
    assert t % tm == 0
    chained = s0 is None
    if chained:
        assert seq_len % tm == 0 and tm % SGU_CHUNK == 0
        tiles_per_seq, seq_per_tile, sgu_c = seq_len // tm, 1, SGU_CHUNK
    else:
        assert seq_len == BLOCK and tm % BLOCK == 0
        tiles_per_seq, seq_per_tile, sgu_c = 1, tm // BLOCK, BLOCK
    lconst3 = lambda i: (layer, 0, 0)
    state_spec = pl.BlockSpec((seq_per_tile, N_HEADS, HEAD, HEAD), lambda i: (i // tiles_per_seq, 0, 0, 0))
    in_specs = [
        pl.BlockSpec((tm, D_MODEL), lambda i: (i, 0)),
        pl.BlockSpec((None, 1, D_MODEL), lconst3),
        pl.BlockSpec((None, D_MODEL, D_IN), lconst3),
        pl.BlockSpec(lb_logits.shape, lambda i: (0, 0)),
        pl.BlockSpec((None, 1, D_HGRN), lconst3),
        pl.BlockSpec((None, 1, D_SGU), lconst3),
        pl.BlockSpec((None, 1, D_SGU), lconst3),
        pl.BlockSpec((None, N_HEADS, SGU_CHUNK, SGU_CHUNK), lambda i: (layer, 0, 0, 0)),
        pl.BlockSpec((None, SGU_CHUNK, D_SGU), lconst3),
        pl.BlockSpec((None, D_MODEL, D_MODEL), lconst3),
    ]
    args = [x, g, w_in, lb_logits, out_gain, ln_g, ln_b, w_s, bias, w_out]
    out_specs = [pl.BlockSpec((tm, D_MODEL), lambda i: (i, 0)), state_spec]
    out_shape = [jax.ShapeDtypeStruct((t, D_MODEL), F32),
                 jax.ShapeDtypeStruct((n_seq, N_HEADS, HEAD, HEAD), F32)]
    if not chained:
        in_specs.append(state_spec)
        args.append(s0)
        out_specs.append(pl.BlockSpec((tm, D_SGU), lambda i: (i, 0)))
        out_shape.append(jax.ShapeDtypeStruct((t, D_SGU), F32))
    wide = lambda dt: pltpu.VMEM((tm, D_HGRN), dt)
    scratch = [wide(F32), wide(F32), wide(F32), wide(F32), wide(F32), wide(F32), wide(BF16),
               pltpu.VMEM((tm, D_MODEL), BF16), pltpu.VMEM((N_HEADS, HEAD, HEAD), F32)]
    return pl.pallas_call(
        functools.partial(_mixer_kernel, layer=layer, tm=tm, chained=chained,
                          tiles_per_seq=tiles_per_seq, sgu_c=sgu_c),
        grid=(t // tm,),
        in_specs=in_specs,
        out_specs=out_specs,
        out_shape=out_shape,
        scratch_shapes=scratch,
        compiler_params=pltpu.CompilerParams(
            dimension_semantics=("arbitrary",), vmem_limit_bytes=VMEM_LIMIT_BYTES),
        name="mixer_chained" if chained else "mixer_stepped",
    )(*args)


def _trunk(x3, s0_stack, p):
    bsz, seq_len, _ = x3.shape
    x = x3.reshape(bsz * seq_len, D_MODEL)
    depth = p["w_in"].shape[0]
    states, vrows = [], []
    for l in range(depth):
        x = _ffn(x, p["norm_ffn1"], p["ffn1_w1"], p["ffn1_w3"], p["ffn1_w2"], l)
        s0 = None if s0_stack is None else s0_stack[l]
        res = _mixer(x, seq_len, s0, p["norm_mix"], p["w_in"], p["lb_logits"], p["hgrn_out_norm"],
                     p["sgu_ln_g"], p["sgu_ln_b"], p["sgu_w_s"], p["sgu_bias"], p["w_out"], l)
        x = res[0]
        states.append(res[1])
        if s0 is not None:
            vrows.append(res[2].reshape(bsz, seq_len, D_SGU))
        x = _ffn(x, p["norm_ffn2"], p["ffn2_w1"], p["ffn2_w3"], p["ffn2_w2"], l,
                 final_g=p["final_norm"] if l == depth - 1 else None)
    y = x.reshape(bsz, seq_len, D_MODEL)
    return y, jnp.stack(states), (jnp.stack(vrows) if vrows else None)


def kernel(x_prompt, x_sample, state_hgrn, lb_logits, norm_ffn1, ffn1_w1, ffn1_w3, ffn1_w2, norm_mix, w_in, hgrn_out_norm, sgu_ln_g, sgu_ln_b, sgu_w_s, sgu_b_s, w_out, norm_ffn2, ffn2_w1, ffn2_w3, ffn2_w2, final_norm):
    row = lambda a: a.reshape(a.shape[0], 1, a.shape[1])
    p = dict(
        lb_logits=lb_logits,
        norm_ffn1=row(norm_ffn1), norm_mix=row(norm_mix), norm_ffn2=row(norm_ffn2),
        hgrn_out_norm=row(hgrn_out_norm), sgu_ln_g=row(sgu_ln_g), sgu_ln_b=row(sgu_ln_b),
        final_norm=final_norm.reshape(1, D_MODEL),
        ffn1_w1=ffn1_w1.astype(BF16), ffn1_w3=ffn1_w3.astype(BF16), ffn1_w2=ffn1_w2.astype(BF16),
        ffn2_w1=ffn2_w1.astype(BF16), ffn2_w3=ffn2_w3.astype(BF16), ffn2_w2=ffn2_w2.astype(BF16),
        w_in=w_in.astype(BF16), w_out=w_out.astype(BF16),
        sgu_w_s=sgu_w_s,
        sgu_bias=jnp.repeat(jnp.swapaxes(sgu_b_s, 1, 2), HEAD, axis=2),
    )
    y_prompt, s_prompt, _ = _trunk(x_prompt, None, p)
    y_sample, s_sample, v_sample = _trunk(x_sample, state_hgrn, p)
    return (y_prompt, y_sample, s_prompt, s_sample, v_sample)
```

```python
import functools

import jax
import jax.numpy as jnp
from jax import lax
from jax.experimental import pallas as pl
from jax.experimental.pallas import tpu as pltpu

F32 = jnp.float32
BF16 = jnp.bfloat16

D_MODEL = 1024
D_FF = 2816
D_HGRN = 512
D_SGU = 512
N_HEADS = 4
HEAD = 128
D_IN = 4 * D_HGRN + 2 * D_SGU
BLOCK = 64
SUBLANES = 8
SUB = 4
GROUP_BLOCKS = 2
SGU_CHUNK = 128
EPS = 1e-6
FFN_HALF = 0.5

TOKEN_TILE = 512
FF_CHUNK = 256
VMEM_LIMIT_BYTES = 56 * 1024 * 1024


def _rms(x, g):
    return x * lax.rsqrt(jnp.mean(x * x, axis=-1, keepdims=True) + EPS) * g


def _dot(a, b):
    return jnp.dot(a, b, preferred_element_type=F32)


def _dot_nt(a, b):
    return lax.dot_general(a, b, (((1,), (1,)), ((), ())), preferred_element_type=F32)


def _dot_tn(a, b):
    return lax.dot_general(a, b, (((0,), (0,)), ((), ())), preferred_element_type=F32)


def _ffn_kernel(*refs, final):
    if final:
        x_ref, g_ref, w1_ref, w3_ref, w2_ref, fg_ref, o_ref, h_ref = refs
    else:
        x_ref, g_ref, w1_ref, w3_ref, w2_ref, o_ref, h_ref = refs
    x = x_ref[...]
    xn = _rms(x, g_ref[...]).astype(BF16)
    for c in range(D_FF // FF_CHUNK):
        cs = slice(c * FF_CHUNK, (c + 1) * FF_CHUNK)
        a = _dot(xn, w1_ref[:, cs])
        b = _dot(xn, w3_ref[:, cs])
        h_ref[:, cs] = (jax.nn.silu(a) * b).astype(BF16)
    out = x + FFN_HALF * _dot(h_ref[...], w2_ref[...])
    if final:
        out = _rms(out, fg_ref[...])
    o_ref[...] = out


def _ffn(x, g, w1, w3, w2, layer, final_g=None):
    t = x.shape[0]
    tm = min(TOKEN_TILE, t)
    assert t % tm == 0
    final = final_g is not None
    const = lambda i: (layer, 0, 0)
    in_specs = [
        pl.BlockSpec((tm, D_MODEL), lambda i: (i, 0)),
        pl.BlockSpec((None, 1, D_MODEL), const),
        pl.BlockSpec((None, D_MODEL, D_FF), const),
        pl.BlockSpec((None, D_MODEL, D_FF), const),
        pl.BlockSpec((None, D_FF, D_MODEL), const),
    ]
    args = [x, g, w1, w3, w2]
    if final:
        in_specs.append(pl.BlockSpec((1, D_MODEL), lambda i: (0, 0)))
        args.append(final_g)
    return pl.pallas_call(
        functools.partial(_ffn_kernel, final=final),
        grid=(t // tm,),
        in_specs=in_specs,
        out_specs=pl.BlockSpec((tm, D_MODEL), lambda i: (i, 0)),
        out_shape=jax.ShapeDtypeStruct((t, D_MODEL), F32),
        scratch_shapes=[pltpu.VMEM((tm, D_FF), BF16)],
        compiler_params=pltpu.CompilerParams(
            dimension_semantics=("arbitrary",), vmem_limit_bytes=VMEM_LIMIT_BYTES),
        name="ffn_final" if final else "ffn",
    )(*args)


def _roll_rows(x, d):
    n, c = x.shape
    return pltpu.roll(x.reshape(n // SUBLANES, SUBLANES, c), d, 1).reshape(n, c)


def _level_operand(q, k, b2, half):
    if half >= SUBLANES:
        pieces = []
        for r0 in range(0, BLOCK, half):
            mid = (r0 // (2 * half)) * 2 * half + half - 1
            if (r0 // half) % 2 == 1:
                pieces.append(q[r0:r0 + half] * jnp.exp2(b2[r0:r0 + half] - b2[mid:mid + 1]))
            else:
                pieces.append(k[r0:r0 + half] * jnp.exp2(b2[mid:mid + 1] - b2[r0:r0 + half]))
        return jnp.concatenate(pieces, axis=0).astype(BF16)
    assert 2 * half == SUBLANES
    b3 = b2.reshape(BLOCK // SUBLANES, SUBLANES, HEAD)
    ref = jnp.broadcast_to(b3[:, half - 1:half, :], b3.shape).reshape(BLOCK, HEAD)
    upper = lax.broadcasted_iota(jnp.int32, (BLOCK, HEAD), 0) % SUBLANES >= half
    return (jnp.where(upper, q, k) * jnp.exp2(-jnp.abs(b2 - ref))).astype(BF16)


def _hgrn_group(qs, ks, vs, b2s, state_in, diag_masks, level_masks):
    n = len(qs)
    chains = range(n)
    vbs = [v.astype(BF16) for v in vs]
    near = [[jnp.sum(qs[g] * ks[g], axis=-1, keepdims=True) for g in chains]]
    for d in range(1, SUB):
        es = [jnp.exp2(jnp.minimum(b2s[g] - _roll_rows(b2s[g], d), 0.0)) for g in chains]
        near.append([jnp.sum(qs[g] * _roll_rows(ks[g], d) * es[g], axis=-1, keepdims=True)
                     for g in chains])
    far = []
    for half, _ in level_masks:
        ws = [_level_operand(qs[g], ks[g], b2s[g], half) for g in chains]
        far.append([_dot_nt(w, w) for w in ws])
    b_last = [b2[BLOCK - 1:BLOCK] for b2 in b2s]
    kdec = [(ks[g] * jnp.exp2(b_last[g] - b2s[g])).astype(BF16) for g in chains]
    grow = [_dot_tn(vbs[g], kdec[g]) for g in chains]
    st_in = [state_in(g, lambda st, g=g: st * jnp.exp2(b_last[g]) + grow[g]) for g in chains]
    qdec = [(qs[g] * jnp.exp2(b2s[g])).astype(BF16) for g in chains]
    o = [_dot_nt(qdec[g], st_in[g].astype(BF16)) for g in chains]
    att = [jnp.where(diag_masks[0], near[0][g], 0.0) for g in chains]
    for d in range(1, SUB):
        att = [jnp.where(diag_masks[d], near[d][g], att[g]) for g in chains]
    for lvl, (_, mask) in enumerate(level_masks):
        att = [jnp.where(mask, far[lvl][g], att[g]) for g in chains]
    return [o[g] + _dot(att[g].astype(BF16), vbs[g]) for g in chains]


def _mixer_kernel(*refs, layer, tm, chained, tiles_per_seq, sgu_c):
    if chained:
        (x_ref, g_ref, win_ref, lbl_ref, og_ref, lng_ref, lnb_ref, ws_ref, bs_ref, wout_ref,
         xo_ref, so_ref,
         q_s, k_s, v_s, lf_s, gate_s, u_s, vn_s, mix_s, st_s) = refs
        s0_ref = vno_ref = None
    else:
        (x_ref, g_ref, win_ref, lbl_ref, og_ref, lng_ref, lnb_ref, ws_ref, bs_ref, wout_ref, s0_ref,
         xo_ref, so_ref, vno_ref,
         q_s, k_s, v_s, lf_s, gate_s, u_s, vn_s, mix_s, st_s) = refs

    hn = _rms(x_ref[...], g_ref[...]).astype(BF16)

    lg = lbl_ref[...]
    e = jnp.exp(lg - jnp.max(lg, axis=0, keepdims=True))
    p = e / jnp.sum(e, axis=0, keepdims=True)
    lb = p[0:1]
    for j in range(1, layer + 1):
        lb = lb + p[j:j + 1]
    lb = lb - p[0:1]

    def zcols(j):
        return _dot(hn, win_ref[:, j * D_HGRN:(j + 1) * D_HGRN])

    q_s[...] = jax.nn.silu(zcols(0)) * (HEAD ** -0.5)
    f = lb + (1.0 - lb) * jax.nn.sigmoid(zcols(1))
    lf_s[...] = jnp.log2(f)
    k_s[...] = 1.0 - f
    v_s[...] = zcols(2)
    gate_s[...] = jax.nn.silu(zcols(3))
    u_s[...] = jax.nn.gelu(zcols(4))
    vg = jax.nn.gelu(zcols(5))
    for h in range(N_HEADS):
        cs = slice(h * HEAD, (h + 1) * HEAD)
        vh = vg[:, cs]
        mu = jnp.mean(vh, axis=-1, keepdims=True)
        var = jnp.mean(jnp.square(vh - mu), axis=-1, keepdims=True)
        vn = (vh - mu) * lax.rsqrt(var + EPS) * lng_ref[:, cs] + lnb_ref[:, cs]
        if vno_ref is not None:
            vno_ref[:, cs] = vn
        vn_s[:, cs] = vn.astype(BF16)

    ti = lax.broadcasted_iota(jnp.int32, (BLOCK, BLOCK), 0)
    si = lax.broadcasted_iota(jnp.int32, (BLOCK, BLOCK), 1)
    tril3 = jnp.concatenate([(ti >= si).astype(BF16)] * 3, axis=1)
    diag_masks = [ti - si == d for d in range(SUB)]
    level_masks = []
    half = BLOCK // 2
    while half >= SUB:
        same = (ti // (2 * half)) == (si // (2 * half))
        level_masks.append((half, same & ((ti % (2 * half)) >= half) & ((si % (2 * half)) < half)))
        half //= 2

    if chained:
        @pl.when(pl.program_id(0) % tiles_per_seq == 0)
        def _():
            st_s[...] = jnp.zeros_like(st_s)
        state = [st_s[h] for h in range(N_HEADS)]

    def cum_log2_decay(i):
        lf = lf_s[i * BLOCK:(i + 1) * BLOCK, :]
        lf_hi = lf.astype(BF16)
        rem = lf - lf_hi.astype(F32)
        lf_mid = rem.astype(BF16)
        lf_lo = (rem - lf_mid.astype(F32)).astype(BF16)
        return _dot(tril3, jnp.concatenate([lf_hi, lf_mid, lf_lo], axis=0))

    for i0 in range(0, tm // BLOCK, GROUP_BLOCKS):
        blocks = range(i0, i0 + GROUP_BLOCKS)
        b2_all = {i: cum_log2_decay(i) for i in blocks}
        ids = [(i, h) for i in blocks for h in range(N_HEADS)]
        rows = [slice(i * BLOCK, (i + 1) * BLOCK) for i, _ in ids]
        cols = [slice(h * HEAD, (h + 1) * HEAD) for _, h in ids]

        def state_in(g, update):
            i, h = ids[g]
            if chained:
                st = state[h]
                state[h] = update(st)
            else:
                st = s0_ref[i, h].T
                so_ref[i, h] = update(st).T
            return st

        outs = _hgrn_group([q_s[r, c] for r, c in zip(rows, cols)],
                           [k_s[r, c] for r, c in zip(rows, cols)],
                           [v_s[r, c] for r, c in zip(rows, cols)],
                           [b2_all[i][:, c] for (i, _), c in zip(ids, cols)],
                           state_in, diag_masks, level_masks)
        ms = [jnp.mean(o * o, axis=-1, keepdims=True) for o in outs]
        for g, (r, c) in enumerate(zip(rows, cols)):
            o = outs[g] * lax.rsqrt(ms[g] + EPS) * og_ref[:, c]
            mix_s[r, c] = (o * gate_s[r, c]).astype(BF16)

    if chained:
        for h in range(N_HEADS):
            st_s[h] = state[h]
            so_ref[0, h] = state[h].T

    tc = lax.broadcasted_iota(jnp.int32, (sgu_c, sgu_c), 0)
    sc = lax.broadcasted_iota(jnp.int32, (sgu_c, sgu_c), 1)
    for h in range(N_HEADS):
        cs = slice(h * HEAD, (h + 1) * HEAD)
        ws = jnp.where(tc >= sc, ws_ref[h, :sgu_c, :sgu_c], 0.0).astype(BF16)
        for j in range(tm // sgu_c):
            rows = slice(j * sgu_c, (j + 1) * sgu_c)
            mixed = _dot(ws, vn_s[rows, cs]) + bs_ref[:sgu_c, cs]
            mix_s[rows, D_HGRN + h * HEAD:D_HGRN + (h + 1) * HEAD] = (u_s[rows, cs] * mixed).astype(BF16)

    xo_ref[...] = x_ref[...] + _dot(mix_s[...], wout_ref[...])


def _mixer(x, seq_len, s0, g, w_in, lb_logits, out_gain, ln_g, ln_b, w_s, bias, w_out, layer):
    t = x.shape[0]
    n_seq = t // seq_len
    tm = min(TOKEN_TILE, t)
    assert t % tm == 0
    chained = s0 is None
    if chained:
        assert seq_len % tm == 0 and tm % SGU_CHUNK == 0
        tiles_per_seq, seq_per_tile, sgu_c = seq_len // tm, 1, SGU_CHUNK
    else:
        assert seq_len == BLOCK and tm % BLOCK == 0
        tiles_per_seq, seq_per_tile, sgu_c = 1, tm // BLOCK, BLOCK
    lconst3 = lambda i: (layer, 0, 0)
    state_spec = pl.BlockSpec((seq_per_tile, N_HEADS, HEAD, HEAD), lambda i: (i // tiles_per_seq, 0, 0, 0))
    in_specs = [
        pl.BlockSpec((tm, D_MODEL), lambda i: (i, 0)),
        pl.BlockSpec((None, 1, D_MODEL), lconst3),
        pl.BlockSpec((None, D_MODEL, D_IN), lconst3),
        pl.BlockSpec(lb_logits.shape, lambda i: (0, 0)),
        pl.BlockSpec((None, 1, D_HGRN), lconst3),
        pl.BlockSpec((None, 1, D_SGU), lconst3),
        pl.BlockSpec((None, 1, D_SGU), lconst3),
        pl.BlockSpec((None, N_HEADS, SGU_CHUNK, SGU_CHUNK), lambda i: (layer, 0, 0, 0)),
        pl.BlockSpec((None, SGU_CHUNK, D_SGU), lconst3),
        pl.BlockSpec((None, D_MODEL, D_MODEL), lconst3),
    ]
    args = [x, g, w_in, lb_logits, out_gain, ln_g, ln_b, w_s, bias, w_out]
    out_specs = [pl.BlockSpec((tm, D_MODEL), lambda i: (i, 0)), state_spec]
    out_shape = [jax.ShapeDtypeStruct((t, D_MODEL), F32),
                 jax.ShapeDtypeStruct((n_seq, N_HEADS, HEAD, HEAD), F32)]
    if not chained:
        in_specs.append(state_spec)
        args.append(s0)
        out_specs.append(pl.BlockSpec((tm, D_SGU), lambda i: (i, 0)))
        out_shape.append(jax.ShapeDtypeStruct((t, D_SGU), F32))
    wide = lambda dt: pltpu.VMEM((tm, D_HGRN), dt)
    scratch = [wide(F32), wide(F32), wide(F32), wide(F32), wide(F32), wide(F32), wide(BF16),
               pltpu.VMEM((tm, D_MODEL), BF16), pltpu.VMEM((N_HEADS, HEAD, HEAD), F32)]
    return pl.pallas_call(
        functools.partial(_mixer_kernel, layer=layer, tm=tm, chained=chained,
                          tiles_per_seq=tiles_per_seq, sgu_c=sgu_c),
        grid=(t // tm,),
        in_specs=in_specs,
        out_specs=out_specs,
        out_shape=out_shape,
        scratch_shapes=scratch,
        compiler_params=pltpu.CompilerParams(
            dimension_semantics=("arbitrary",), vmem_limit_bytes=VMEM_LIMIT_BYTES),
        name="mixer_chained" if chained else "mixer_stepped",
    )(*args)


def _trunk(x3, s0_stack, p):
    bsz, seq_len, _ = x3.shape
    x = x3.reshape(bsz * seq_len, D_MODEL)
    depth = p["w_in"].shape[0]
    states, vrows = [], []
    for l in range(depth):
        x = _ffn(x, p["norm_ffn1"], p["ffn1_w1"], p["ffn1_w3"], p["ffn1_w2"], l)
        s0 = None if s0_stack is None else s0_stack[l]
        res = _mixer(x, seq_len, s0, p["norm_mix"], p["w_in"], p["lb_logits"], p["hgrn_out_norm"],
                     p["sgu_ln_g"], p["sgu_ln_b"], p["sgu_w_s"], p["sgu_bias"], p["w_out"], l)
        x = res[0]
        states.append(res[1])
        if s0 is not None:
            vrows.append(res[2].reshape(bsz, seq_len, D_SGU))
        x = _ffn(x, p["norm_ffn2"], p["ffn2_w1"], p["ffn2_w3"], p["ffn2_w2"], l,
                 final_g=p["final_norm"] if l == depth - 1 else None)
    y = x.reshape(bsz, seq_len, D_MODEL)
    return y, jnp.stack(states), (jnp.stack(vrows) if vrows else None)


def kernel(x_prompt, x_sample, state_hgrn, lb_logits, norm_ffn1, ffn1_w1, ffn1_w3, ffn1_w2, norm_mix, w_in, hgrn_out_norm, sgu_ln_g, sgu_ln_b, sgu_w_s, sgu_b_s, w_out, norm_ffn2, ffn2_w1, ffn2_w3, ffn2_w2, final_norm):
    row = lambda a: a.reshape(a.shape[0], 1, a.shape[1])
    p = dict(
        lb_logits=lb_logits,
        norm_ffn1=row(norm_ffn1), norm_mix=row(norm_mix), norm_ffn2=row(norm_ffn2),
        hgrn_out_norm=row(hgrn_out_norm), sgu_ln_g=row(sgu_ln_g), sgu_ln_b=row(sgu_ln_b),
        final_norm=final_norm.reshape(1, D_MODEL),
        ffn1_w1=ffn1_w1.astype(BF16), ffn1_w3=ffn1_w3.astype(BF16), ffn1_w2=ffn1_w2.astype(BF16),
        ffn2_w1=ffn2_w1.astype(BF16), ffn2_w3=ffn2_w3.astype(BF16), ffn2_w2=ffn2_w2.astype(BF16),
        w_in=w_in.astype(BF16), w_out=w_out.astype(BF16),
        sgu_w_s=sgu_w_s,
        sgu_bias=jnp.repeat(jnp.swapaxes(sgu_b_s, 1, 2), HEAD, axis=2),
    )
    y_prompt, s_prompt, _ = _trunk(x_prompt, None, p)
    y_sample, s_sample, v_sample = _trunk(x_sample, state_hgrn, p)
    return (y_prompt, y_sample, s_prompt, s_sample, v_sample)
```

```python
import functools

import jax
import jax.numpy as jnp
from jax import lax
from jax.experimental import pallas as pl
from jax.experimental.pallas import tpu as pltpu

F32 = jnp.float32
BF16 = jnp.bfloat16

D_MODEL = 1024
D_FF = 2816
D_HGRN = 512
D_SGU = 512
N_HEADS = 4
HEAD = 128
D_IN = 4 * D_HGRN + 2 * D_SGU
BLOCK = 64
SUBLANES = 8
SUB = 4
GROUP_BLOCKS = 2
SGU_CHUNK = 128
EPS = 1e-6
FFN_HALF = 0.5

TOKEN_TILE = 512
FF_CHUNK = 256
A_COLS = 256
VMEM_LIMIT_BYTES = 56 * 1024 * 1024


def _rms(x, g):
    return x * lax.rsqrt(jnp.mean(x * x, axis=-1, keepdims=True) + EPS) * g


def _dot(a, b):
    return jnp.dot(a, b, preferred_element_type=F32)


def _dot_nt(a, b):
    return lax.dot_general(a, b, (((1,), (1,)), ((), ())), preferred_element_type=F32)


def _dot_tn(a, b):
    return lax.dot_general(a, b, (((0,), (0,)), ((), ())), preferred_element_type=F32)


def _ffn_kernel(*refs, final):
    if final:
        x_ref, g_ref, w1_ref, w3_ref, w2_ref, fg_ref, o_ref, h_ref = refs
    else:
        x_ref, g_ref, w1_ref, w3_ref, w2_ref, o_ref, h_ref = refs
    x = x_ref[...]
    xn = _rms(x, g_ref[...]).astype(BF16)
    for c in range(D_FF // FF_CHUNK):
        cs = slice(c * FF_CHUNK, (c + 1) * FF_CHUNK)
        a = _dot(xn, w1_ref[:, cs])
        b = _dot(xn, w3_ref[:, cs])
        h_ref[:, cs] = (jax.nn.silu(a) * b).astype(BF16)
    out = x + FFN_HALF * _dot(h_ref[...], w2_ref[...])
    if final:
        out = _rms(out, fg_ref[...])
    o_ref[...] = out


def _ffn(x, g, w1, w3, w2, layer, final_g=None):
    t = x.shape[0]
    tm = min(TOKEN_TILE, t)
    assert t % tm == 0
    final = final_g is not None
    const = lambda i: (layer, 0, 0)
    in_specs = [
        pl.BlockSpec((tm, D_MODEL), lambda i: (i, 0)),
        pl.BlockSpec((None, 1, D_MODEL), const),
        pl.BlockSpec((None, D_MODEL, D_FF), const),
        pl.BlockSpec((None, D_MODEL, D_FF), const),
        pl.BlockSpec((None, D_FF, D_MODEL), const),
    ]
    args = [x, g, w1, w3, w2]
    if final:
        in_specs.append(pl.BlockSpec((1, D_MODEL), lambda i: (0, 0)))
        args.append(final_g)
    return pl.pallas_call(
        functools.partial(_ffn_kernel, final=final),
        grid=(t // tm,),
        in_specs=in_specs,
        out_specs=pl.BlockSpec((tm, D_MODEL), lambda i: (i, 0)),
        out_shape=jax.ShapeDtypeStruct((t, D_MODEL), F32),
        scratch_shapes=[pltpu.VMEM((tm, D_FF), BF16)],
        compiler_params=pltpu.CompilerParams(
            dimension_semantics=("arbitrary",), vmem_limit_bytes=VMEM_LIMIT_BYTES),
        name="ffn_final" if final else "ffn",
    )(*args)


def _roll_rows(x, d):
    n, c = x.shape
    return pltpu.roll(x.reshape(n // SUBLANES, SUBLANES, c), d, 1).reshape(n, c)


def _level_operand(q, k, b2, half):
    if half >= SUBLANES:
        pieces = []
        for r0 in range(0, BLOCK, half):
            mid = (r0 // (2 * half)) * 2 * half + half - 1
            if (r0 // half) % 2 == 1:
                pieces.append(q[r0:r0 + half] * jnp.exp2(b2[r0:r0 + half] - b2[mid:mid + 1]))
            else:
                pieces.append(k[r0:r0 + half] * jnp.exp2(b2[mid:mid + 1] - b2[r0:r0 + half]))
        return jnp.concatenate(pieces, axis=0).astype(BF16)
    assert 2 * half == SUBLANES
    b3 = b2.reshape(BLOCK // SUBLANES, SUBLANES, HEAD)
    ref = jnp.broadcast_to(b3[:, half - 1:half, :], b3.shape).reshape(BLOCK, HEAD)
    upper = lax.broadcasted_iota(jnp.int32, (BLOCK, HEAD), 0) % SUBLANES >= half
    return (jnp.where(upper, q, k) * jnp.exp2(-jnp.abs(b2 - ref))).astype(BF16)


def _hgrn_group(load, state_in, finish, diag_masks, level_masks):
    qs, ks, vs, b2s = load()
    n = len(qs)
    chains = range(n)
    vbs = [v.astype(BF16) for v in vs]
    near = [[jnp.sum(qs[g] * ks[g], axis=-1, keepdims=True) for g in chains]]
    for d in range(1, SUB):
        es = [jnp.exp2(jnp.minimum(b2s[g] - _roll_rows(b2s[g], d), 0.0)) for g in chains]
        near.append([jnp.sum(qs[g] * _roll_rows(ks[g], d) * es[g], axis=-1, keepdims=True)
                     for g in chains])
    yield
    far = []
    for half, _ in level_masks:
        ws = [_level_operand(qs[g], ks[g], b2s[g], half) for g in chains]
        far.append([_dot_nt(w, w) for w in ws])
    yield
    b_last = [b2[BLOCK - 1:BLOCK] for b2 in b2s]
    kdec = [(ks[g] * jnp.exp2(b_last[g] - b2s[g])).astype(BF16) for g in chains]
    grow = [_dot_tn(vbs[g], kdec[g]) for g in chains]
    st_in = [state_in(g, lambda st, g=g: st * jnp.exp2(b_last[g]) + grow[g]) for g in chains]
    qdec = [(qs[g] * jnp.exp2(b2s[g])).astype(BF16) for g in chains]
    o = [_dot_nt(qdec[g], st_in[g].astype(BF16)) for g in chains]
    yield
    att = [jnp.where(diag_masks[0], near[0][g], 0.0) for g in chains]
    for d in range(1, SUB):
        att = [jnp.where(diag_masks[d], near[d][g], att[g]) for g in chains]
    for lvl, (_, mask) in enumerate(level_masks):
        att = [jnp.where(mask, far[lvl][g], att[g]) for g in chains]
    finish([o[g] + _dot(att[g].astype(BF16), vbs[g]) for g in chains])
    yield


def _mixer_kernel(*refs, layer, tm, n_tiles, chained, tiles_per_seq, sgu_c):
    if chained:
        (xa_ref, xb_ref, g_ref, win_ref, lbl_ref, og_ref, lng_ref, lnb_ref, ws_ref, bs_ref, wout_ref,
         xo_ref, so_ref,
         q_s, k_s, v_s, lf_s, gate_s, u_s, vn_s, mix_s, st_s) = refs
        s0_ref = vno_ref = None
    else:
        (xa_ref, xb_ref, g_ref, win_ref, lbl_ref, og_ref, lng_ref, lnb_ref, ws_ref, bs_ref, wout_ref,
         s0_ref,
         xo_ref, so_ref, vno_ref,
         q_s, k_s, v_s, lf_s, gate_s, u_s, vn_s, mix_s, st_s) = refs
    step = pl.program_id(0)
    tile_b = jnp.maximum(step - 1, 0)

    @pl.when(step == 0)
    def _():
        for ref in (q_s, k_s, v_s, lf_s, gate_s, u_s, vn_s):
            ref[1] = jnp.zeros(ref.shape[1:], ref.dtype)

    if chained:
        @pl.when(tile_b % tiles_per_seq == 0)
        def _():
            st_s[...] = jnp.zeros_like(st_s)

    def both_stages(slot_a, slot_b):
        state = [st_s[h] for h in range(N_HEADS)] if chained else None

        hn = _rms(xa_ref[...], g_ref[...]).astype(BF16)

        lg = lbl_ref[...]
        e = jnp.exp(lg - jnp.max(lg, axis=0, keepdims=True))
        p = e / jnp.sum(e, axis=0, keepdims=True)
        lb = p[0:1]
        for j in range(1, layer + 1):
            lb = lb + p[j:j + 1]
        lb = lb - p[0:1]

        def a_piece(group, half):
            cs = slice(half * A_COLS, (half + 1) * A_COLS)
            z = _dot(hn, win_ref[:, group * D_HGRN + half * A_COLS:group * D_HGRN + (half + 1) * A_COLS])
            if group == 0:
                q_s[slot_a, :, cs] = jax.nn.silu(z) * (HEAD ** -0.5)
            elif group == 1:
                f = lb[:, cs] + (1.0 - lb[:, cs]) * jax.nn.sigmoid(z)
                lf_s[slot_a, :, cs] = jnp.log2(f)
                k_s[slot_a, :, cs] = 1.0 - f
            elif group == 2:
                v_s[slot_a, :, cs] = z
            elif group == 3:
                gate_s[slot_a, :, cs] = jax.nn.silu(z)
            elif group == 4:
                u_s[slot_a, :, cs] = jax.nn.gelu(z)
            else:
                vg = jax.nn.gelu(z)
                for h in range(A_COLS // HEAD):
                    hs = slice(h * HEAD, (h + 1) * HEAD)
                    os_ = slice(half * A_COLS + h * HEAD, half * A_COLS + (h + 1) * HEAD)
                    vh = vg[:, hs]
                    mu = jnp.mean(vh, axis=-1, keepdims=True)
                    var = jnp.mean(jnp.square(vh - mu), axis=-1, keepdims=True)
                    vn = (vh - mu) * lax.rsqrt(var + EPS) * lng_ref[:, os_] + lnb_ref[:, os_]
                    if vno_ref is not None:
                        vno_ref[:, os_] = vn
                    vn_s[slot_a, :, os_] = vn.astype(BF16)

        a_pieces = [functools.partial(a_piece, group, half)
                    for group in range(D_IN // D_HGRN) for half in range(D_HGRN // A_COLS)]

        ti = lax.broadcasted_iota(jnp.int32, (BLOCK, BLOCK), 0)
        si = lax.broadcasted_iota(jnp.int32, (BLOCK, BLOCK), 1)
        tril3 = jnp.concatenate([(ti >= si).astype(BF16)] * 3, axis=1)
        diag_masks = [ti - si == d for d in range(SUB)]
        level_masks = []
        half = BLOCK // 2
        while half >= SUB:
            same = (ti // (2 * half)) == (si // (2 * half))
            level_masks.append((half, same & ((ti % (2 * half)) >= half) & ((si % (2 * half)) < half)))
            half //= 2

        def cum_log2_decay(i):
            lf = lf_s[slot_b, i * BLOCK:(i + 1) * BLOCK, :]
            lf_hi = lf.astype(BF16)
            rem = lf - lf_hi.astype(F32)
            lf_mid = rem.astype(BF16)
            lf_lo = (rem - lf_mid.astype(F32)).astype(BF16)
            return _dot(tril3, jnp.concatenate([lf_hi, lf_mid, lf_lo], axis=0))

        def b_hgrn(i0):
            blocks = range(i0, i0 + GROUP_BLOCKS)
            ids = [(i, h) for i in blocks for h in range(N_HEADS)]
            rows = [slice(i * BLOCK, (i + 1) * BLOCK) for i, _ in ids]
            cols = [slice(h * HEAD, (h + 1) * HEAD) for _, h in ids]

            def load():
                b2_all = {i: cum_log2_decay(i) for i in blocks}
                return ([q_s[slot_b, r, c] for r, c in zip(rows, cols)],
                        [k_s[slot_b, r, c] for r, c in zip(rows, cols)],
                        [v_s[slot_b, r, c] for r, c in zip(rows, cols)],
                        [b2_all[i][:, c] for (i, _), c in zip(ids, cols)])

            def state_in(g, update):
                i, h = ids[g]
                if chained:
                    st = state[h]
                    state[h] = update(st)
                else:
                    st = s0_ref[i, h].T
                    so_ref[i, h] = update(st).T
                return st

            def finish(outs):
                ms = [jnp.mean(o * o, axis=-1, keepdims=True) for o in outs]
                for g, (r, c) in enumerate(zip(rows, cols)):
                    o = outs[g] * lax.rsqrt(ms[g] + EPS) * og_ref[:, c]
                    mix_s[r, c] = (o * gate_s[slot_b, r, c]).astype(BF16)

            return _hgrn_group(load, state_in, finish, diag_masks, level_masks)

        def b_tail():
            tc = lax.broadcasted_iota(jnp.int32, (sgu_c, sgu_c), 0)
            sc = lax.broadcasted_iota(jnp.int32, (sgu_c, sgu_c), 1)
            for h in range(N_HEADS):
                cs = slice(h * HEAD, (h + 1) * HEAD)
                ws = jnp.where(tc >= sc, ws_ref[h, :sgu_c, :sgu_c], 0.0).astype(BF16)
                for j in range(tm // sgu_c):
                    rows = slice(j * sgu_c, (j + 1) * sgu_c)
                    mixed = _dot(ws, vn_s[slot_b, rows, cs]) + bs_ref[:sgu_c, cs]
                    mix_s[rows, D_HGRN + h * HEAD:D_HGRN + (h + 1) * HEAD] = (
                        u_s[slot_b, rows, cs] * mixed).astype(BF16)
                yield
            xo_ref[...] = xb_ref[...] + _dot(mix_s[...], wout_ref[...])
            yield

        pending = iter(a_pieces)
        segments = [b_hgrn(i0) for i0 in range(0, tm // BLOCK, GROUP_BLOCKS)] + [b_tail()]
        for segment in segments:
            for _ in segment:
                piece = next(pending, None)
                if piece is not None:
                    piece()
        for piece in pending:
            piece()

        if chained:
            for h in range(N_HEADS):
                st_s[h] = state[h]
                so_ref[0, h] = state[h].T

    @pl.when(step % 2 == 0)
    def _():
        both_stages(0, 1)

    @pl.when(step % 2 == 1)
    def _():
        both_stages(1, 0)


def _mixer(x, seq_len, s0, g, w_in, lb_logits, out_gain, ln_g, ln_b, w_s, bias, w_out, layer):
    t = x.shape[0]
    n_seq = t // seq_len
    tm = min(TOKEN_TILE, t)
    assert t % tm == 0 and (tm // BLOCK) % GROUP_BLOCKS == 0
    n_tiles = t // tm
    chained = s0 is None
    if chained:
        assert seq_len % tm == 0 and tm % SGU_CHUNK == 0
        tiles_per_seq, seq_per_tile, sgu_c = seq_len // tm, 1, SGU_CHUNK
    else:
        assert seq_len == BLOCK and tm % BLOCK == 0
        tiles_per_seq, seq_per_tile, sgu_c = 1, tm // BLOCK, BLOCK
    lconst3 = lambda i: (layer, 0, 0)
    tile_a = lambda i: jnp.minimum(i, n_tiles - 1)
    tile_b = lambda i: jnp.maximum(i - 1, 0)
    state_spec = pl.BlockSpec((seq_per_tile, N_HEADS, HEAD, HEAD),
                              lambda i: (tile_b(i) // tiles_per_seq, 0, 0, 0))
    in_specs = [
        pl.BlockSpec((tm, D_MODEL), lambda i: (tile_a(i), 0)),
        pl.BlockSpec((tm, D_MODEL), lambda i: (tile_b(i), 0)),
        pl.BlockSpec((None, 1, D_MODEL), lconst3),
        pl.BlockSpec((None, D_MODEL, D_IN), lconst3),
        pl.BlockSpec(lb_logits.shape, lambda i: (0, 0)),
        pl.BlockSpec((None, 1, D_HGRN), lconst3),
        pl.BlockSpec((None, 1, D_SGU), lconst3),
        pl.BlockSpec((None, 1, D_SGU), lconst3),
        pl.BlockSpec((None, N_HEADS, SGU_CHUNK, SGU_CHUNK), lambda i: (layer, 0, 0, 0)),
        pl.BlockSpec((None, SGU_CHUNK, D_SGU), lconst3),
        pl.BlockSpec((None, D_MODEL, D_MODEL), lconst3),
    ]
    args = [x, x, g, w_in, lb_logits, out_gain, ln_g, ln_b, w_s, bias, w_out]
    out_specs = [pl.BlockSpec((tm, D_MODEL), lambda i: (tile_b(i), 0)), state_spec]
    out_shape = [jax.ShapeDtypeStruct((t, D_MODEL), F32),
                 jax.ShapeDtypeStruct((n_seq, N_HEADS, HEAD, HEAD), F32)]
    if not chained:
        in_specs.append(state_spec)
        args.append(s0)
        out_specs.append(pl.BlockSpec((tm, D_SGU), lambda i: (tile_a(i), 0)))
        out_shape.append(jax.ShapeDtypeStruct((t, D_SGU), F32))
    wide = lambda dt: pltpu.VMEM((2, tm, D_HGRN), dt)
    scratch = [wide(F32), wide(F32), wide(F32), wide(F32), wide(F32), wide(F32), wide(BF16),
               pltpu.VMEM((tm, D_MODEL), BF16), pltpu.VMEM((N_HEADS, HEAD, HEAD), F32)]
    return pl.pallas_call(
        functools.partial(_mixer_kernel, layer=layer, tm=tm, n_tiles=n_tiles, chained=chained,
                          tiles_per_seq=tiles_per_seq, sgu_c=sgu_c),
        grid=(n_tiles + 1,),
        in_specs=in_specs,
        out_specs=out_specs,
        out_shape=out_shape,
        scratch_shapes=scratch,
        compiler_params=pltpu.CompilerParams(
            dimension_semantics=("arbitrary",), vmem_limit_bytes=VMEM_LIMIT_BYTES),
        name="mixer_chained" if chained else "mixer_stepped",
    )(*args)


def _trunk(x3, s0_stack, p):
    bsz, seq_len, _ = x3.shape
    x = x3.reshape(bsz * seq_len, D_MODEL)
    depth = p["w_in"].shape[0]
    states, vrows = [], []
    for l in range(depth):
        x = _ffn(x, p["norm_ffn1"], p["ffn1_w1"], p["ffn1_w3"], p["ffn1_w2"], l)
        s0 = None if s0_stack is None else s0_stack[l]
        res = _mixer(x, seq_len, s0, p["norm_mix"], p["w_in"], p["lb_logits"], p["hgrn_out_norm"],
                     p["sgu_ln_g"], p["sgu_ln_b"], p["sgu_w_s"], p["sgu_bias"], p["w_out"], l)
        x = res[0]
        states.append(res[1])
        if s0 is not None:
            vrows.append(res[2].reshape(bsz, seq_len, D_SGU))
        x = _ffn(x, p["norm_ffn2"], p["ffn2_w1"], p["ffn2_w3"], p["ffn2_w2"], l,
                 final_g=p["final_norm"] if l == depth - 1 else None)
    y = x.reshape(bsz, seq_len, D_MODEL)
    return y, jnp.stack(states), (jnp.stack(vrows) if vrows else None)


def kernel(x_prompt, x_sample, state_hgrn, lb_logits, norm_ffn1, ffn1_w1, ffn1_w3, ffn1_w2, norm_mix, w_in, hgrn_out_norm, sgu_ln_g, sgu_ln_b, sgu_w_s, sgu_b_s, w_out, norm_ffn2, ffn2_w1, ffn2_w3, ffn2_w2, final_norm):
    row = lambda a: a.reshape(a.shape[0], 1, a.shape[1])
    p = dict(
        lb_logits=lb_logits,
        norm_ffn1=row(norm_ffn1), norm_mix=row(norm_mix), norm_ffn2=row(norm_ffn2),
        hgrn_out_norm=row(hgrn_out_norm), sgu_ln_g=row(sgu_ln_g), sgu_ln_b=row(sgu_ln_b),
        final_norm=final_norm.reshape(1, D_MODEL),
        ffn1_w1=ffn1_w1.astype(BF16), ffn1_w3=ffn1_w3.astype(BF16), ffn1_w2=ffn1_w2.astype(BF16),
        ffn2_w1=ffn2_w1.astype(BF16), ffn2_w3=ffn2_w3.astype(BF16), ffn2_w2=ffn2_w2.astype(BF16),
        w_in=w_in.astype(BF16), w_out=w_out.astype(BF16),
        sgu_w_s=sgu_w_s,
        sgu_bias=jnp.repeat(jnp.swapaxes(sgu_b_s, 1, 2), HEAD, axis=2),
    )
    y_prompt, s_prompt, _ = _trunk(x_prompt, None, p)
    y_sample, s_sample, v_sample = _trunk(x_sample, state_hgrn, p)
    return (y_prompt, y_sample, s_prompt, s_sample, v_sample)
```

```python
import functools

import jax
import jax.numpy as jnp
from jax import lax
from jax.experimental import pallas as pl
from jax.experimental.pallas import tpu as pltpu

F32 = jnp.float32
BF16 = jnp.bfloat16

D_MODEL = 1024
D_FF = 2816
D_HGRN = 512
D_SGU = 512
N_HEADS = 4
HEAD = 128
D_IN = 4 * D_HGRN + 2 * D_SGU
BLOCK = 64
SUBLANES = 8
SUB = 4
GROUP_BLOCKS = 2
SGU_CHUNK = 128
EPS = 1e-6
FFN_HALF = 0.5

TOKEN_TILE = 512
FFN_PARTS = 2
FF_CHUNK = 256
A_COLS = 256
VMEM_LIMIT_BYTES = 56 * 1024 * 1024


def _rms(x, g):
    return x * lax.rsqrt(jnp.mean(x * x, axis=-1, keepdims=True) + EPS) * g


def _dot(a, b):
    return jnp.dot(a, b, preferred_element_type=F32)


def _dot_nt(a, b):
    return lax.dot_general(a, b, (((1,), (1,)), ((), ())), preferred_element_type=F32)


def _dot_tn(a, b):
    return lax.dot_general(a, b, (((0,), (0,)), ((), ())), preferred_element_type=F32)


def _ffn_kernel(*refs, final):
    if final:
        x_ref, g_ref, w1_ref, w3_ref, w2_ref, fg_ref, o_ref, h_s = refs
    else:
        x_ref, g_ref, w1_ref, w3_ref, w2_ref, o_ref, h_s = refs
    part_rows = x_ref.shape[0] // FFN_PARTS
    for part in range(FFN_PARTS):
        rows = slice(part * part_rows, (part + 1) * part_rows)
        x = x_ref[rows, :]
        xn = _rms(x, g_ref[...]).astype(BF16)
        for c in range(D_FF // FF_CHUNK):
            cs = slice(c * FF_CHUNK, (c + 1) * FF_CHUNK)
            a = _dot(xn, w1_ref[:, cs])
            b = _dot(xn, w3_ref[:, cs])
            h_s[rows, cs] = (jax.nn.silu(a) * b).astype(BF16)
        out = x + FFN_HALF * _dot(h_s[rows, :], w2_ref[...])
        if final:
            out = _rms(out, fg_ref[...])
        o_ref[rows, :] = out


def _ffn(x, g, w1, w3, w2, layer, final_g=None):
    t = x.shape[0]
    tm = min(FFN_PARTS * TOKEN_TILE, t)
    assert t % tm == 0 and tm % FFN_PARTS == 0
    final = final_g is not None
    const = lambda i: (layer, 0, 0)
    resident = dict(pipeline_mode=pl.Buffered(1))
    in_specs = [
        pl.BlockSpec((tm, D_MODEL), lambda i: (i, 0)),
        pl.BlockSpec((None, 1, D_MODEL), const),
        pl.BlockSpec((None, D_MODEL, D_FF), const, **resident),
        pl.BlockSpec((None, D_MODEL, D_FF), const, **resident),
        pl.BlockSpec((None, D_FF, D_MODEL), const, **resident),
    ]
    args = [x, g, w1, w3, w2]
    if final:
        in_specs.append(pl.BlockSpec((1, D_MODEL), lambda i: (0, 0)))
        args.append(final_g)
    return pl.pallas_call(
        functools.partial(_ffn_kernel, final=final),
        grid=(t // tm,),
        in_specs=in_specs,
        out_specs=pl.BlockSpec((tm, D_MODEL), lambda i: (i, 0)),
        out_shape=jax.ShapeDtypeStruct((t, D_MODEL), F32),
        scratch_shapes=[pltpu.VMEM((tm, D_FF), BF16)],
        compiler_params=pltpu.CompilerParams(
            dimension_semantics=("arbitrary",), vmem_limit_bytes=VMEM_LIMIT_BYTES),
        name="ffn_final" if final else "ffn",
    )(*args)


def _roll_rows(x, d):
    n, c = x.shape
    return pltpu.roll(x.reshape(n // SUBLANES, SUBLANES, c), d, 1).reshape(n, c)


def _level_operand(q, k, b2, half):
    if half >= SUBLANES:
        pieces = []
        for r0 in range(0, BLOCK, half):
            mid = (r0 // (2 * half)) * 2 * half + half - 1
            if (r0 // half) % 2 == 1:
                pieces.append(q[r0:r0 + half] * jnp.exp2(b2[r0:r0 + half] - b2[mid:mid + 1]))
            else:
                pieces.append(k[r0:r0 + half] * jnp.exp2(b2[mid:mid + 1] - b2[r0:r0 + half]))
        return jnp.concatenate(pieces, axis=0).astype(BF16)
    assert 2 * half == SUBLANES
    b3 = b2.reshape(BLOCK // SUBLANES, SUBLANES, HEAD)
    ref = jnp.broadcast_to(b3[:, half - 1:half, :], b3.shape).reshape(BLOCK, HEAD)
    upper = lax.broadcasted_iota(jnp.int32, (BLOCK, HEAD), 0) % SUBLANES >= half
    return (jnp.where(upper, q, k) * jnp.exp2(-jnp.abs(b2 - ref))).astype(BF16)


def _hgrn_group(load, state_in, finish, diag_masks, level_masks):
    qs, ks, vs, b2s = load()
    n = len(qs)
    chains = range(n)
    vbs = [v.astype(BF16) for v in vs]
    near = [[jnp.sum(qs[g] * ks[g], axis=-1, keepdims=True) for g in chains]]
    for d in range(1, SUB):
        es = [jnp.exp2(jnp.minimum(b2s[g] - _roll_rows(b2s[g], d), 0.0)) for g in chains]
        near.append([jnp.sum(qs[g] * _roll_rows(ks[g], d) * es[g], axis=-1, keepdims=True)
                     for g in chains])
    yield
    far = []
    for half, _ in level_masks:
        ws = [_level_operand(qs[g], ks[g], b2s[g], half) for g in chains]
        far.append([_dot_nt(w, w) for w in ws])
    yield
    b_last = [b2[BLOCK - 1:BLOCK] for b2 in b2s]
    kdec = [(ks[g] * jnp.exp2(b_last[g] - b2s[g])).astype(BF16) for g in chains]
    grow = [_dot_tn(vbs[g], kdec[g]) for g in chains]
    st_in = [state_in(g, lambda st, g=g: st * jnp.exp2(b_last[g]) + grow[g]) for g in chains]
    qdec = [(qs[g] * jnp.exp2(b2s[g])).astype(BF16) for g in chains]
    o = [_dot_nt(qdec[g], st_in[g].astype(BF16)) for g in chains]
    yield
    att = [jnp.where(diag_masks[0], near[0][g], 0.0) for g in chains]
    for d in range(1, SUB):
        att = [jnp.where(diag_masks[d], near[d][g], att[g]) for g in chains]
    for lvl, (_, mask) in enumerate(level_masks):
        att = [jnp.where(mask, far[lvl][g], att[g]) for g in chains]
    finish([o[g] + _dot(att[g].astype(BF16), vbs[g]) for g in chains])
    yield


def _mixer_kernel(*refs, layer, tm, n_tiles, chained, tiles_per_seq, sgu_c):
    if chained:
        (xa_ref, xb_ref, g_ref, win_ref, lbl_ref, og_ref, lng_ref, lnb_ref, ws_ref, bs_ref, wout_ref,
         xo_ref, so_ref,
         q_s, k_s, v_s, lf_s, gate_s, u_s, vn_s, mix_s, st_s) = refs
        s0_ref = vno_ref = None
    else:
        (xa_ref, xb_ref, g_ref, win_ref, lbl_ref, og_ref, lng_ref, lnb_ref, ws_ref, bs_ref, wout_ref,
         s0_ref,
         xo_ref, so_ref, vno_ref,
         q_s, k_s, v_s, lf_s, gate_s, u_s, vn_s, mix_s, st_s) = refs
    step = pl.program_id(0)
    tile_b = jnp.maximum(step - 1, 0)

    @pl.when(step == 0)
    def _():
        for ref in (q_s, k_s, v_s, lf_s, gate_s, u_s, vn_s):
            ref[1] = jnp.zeros(ref.shape[1:], ref.dtype)

    if chained:
        @pl.when(tile_b % tiles_per_seq == 0)
        def _():
            st_s[...] = jnp.zeros_like(st_s)

    def both_stages(slot_a, slot_b):
        state = [st_s[h] for h in range(N_HEADS)] if chained else None

        hn = _rms(xa_ref[...], g_ref[...]).astype(BF16)

        lg = lbl_ref[...]
        e = jnp.exp(lg - jnp.max(lg, axis=0, keepdims=True))
        p = e / jnp.sum(e, axis=0, keepdims=True)
        lb = p[0:1]
        for j in range(1, layer + 1):
            lb = lb + p[j:j + 1]
        lb = lb - p[0:1]

        def a_piece(group, half):
            cs = slice(half * A_COLS, (half + 1) * A_COLS)
            z = _dot(hn, win_ref[:, group * D_HGRN + half * A_COLS:group * D_HGRN + (half + 1) * A_COLS])
            if group == 0:
                q_s[slot_a, :, cs] = jax.nn.silu(z) * (HEAD ** -0.5)
            elif group == 1:
                f = lb[:, cs] + (1.0 - lb[:, cs]) * jax.nn.sigmoid(z)
                lf_s[slot_a, :, cs] = jnp.log2(f)
                k_s[slot_a, :, cs] = 1.0 - f
            elif group == 2:
                v_s[slot_a, :, cs] = z
            elif group == 3:
                gate_s[slot_a, :, cs] = jax.nn.silu(z)
            elif group == 4:
                u_s[slot_a, :, cs] = jax.nn.gelu(z)
            else:
                vg = jax.nn.gelu(z)
                for h in range(A_COLS // HEAD):
                    hs = slice(h * HEAD, (h + 1) * HEAD)
                    os_ = slice(half * A_COLS + h * HEAD, half * A_COLS + (h + 1) * HEAD)
                    vh = vg[:, hs]
                    mu = jnp.mean(vh, axis=-1, keepdims=True)
                    var = jnp.mean(jnp.square(vh - mu), axis=-1, keepdims=True)
                    vn = (vh - mu) * lax.rsqrt(var + EPS) * lng_ref[:, os_] + lnb_ref[:, os_]
                    if vno_ref is not None:
                        vno_ref[:, os_] = vn
                    vn_s[slot_a, :, os_] = vn.astype(BF16)

        a_pieces = [functools.partial(a_piece, group, half)
                    for group in range(D_IN // D_HGRN) for half in range(D_HGRN // A_COLS)]

        ti = lax.broadcasted_iota(jnp.int32, (BLOCK, BLOCK), 0)
        si = lax.broadcasted_iota(jnp.int32, (BLOCK, BLOCK), 1)
        tril3 = jnp.concatenate([(ti >= si).astype(BF16)] * 3, axis=1)
        diag_masks = [ti - si == d for d in range(SUB)]
        level_masks = []
        half = BLOCK // 2
        while half >= SUB:
            same = (ti // (2 * half)) == (si // (2 * half))
            level_masks.append((half, same & ((ti % (2 * half)) >= half) & ((si % (2 * half)) < half)))
            half //= 2

        def cum_log2_decay(i):
            lf = lf_s[slot_b, i * BLOCK:(i + 1) * BLOCK, :]
            lf_hi = lf.astype(BF16)
            rem = lf - lf_hi.astype(F32)
            lf_mid = rem.astype(BF16)
            lf_lo = (rem - lf_mid.astype(F32)).astype(BF16)
            return _dot(tril3, jnp.concatenate([lf_hi, lf_mid, lf_lo], axis=0))

        def b_hgrn(i0):
            blocks = range(i0, i0 + GROUP_BLOCKS)
            ids = [(i, h) for i in blocks for h in range(N_HEADS)]
            rows = [slice(i * BLOCK, (i + 1) * BLOCK) for i, _ in ids]
            cols = [slice(h * HEAD, (h + 1) * HEAD) for _, h in ids]

            def load():
                b2_all = {i: cum_log2_decay(i) for i in blocks}
                return ([q_s[slot_b, r, c] for r, c in zip(rows, cols)],
                        [k_s[slot_b, r, c] for r, c in zip(rows, cols)],
                        [v_s[slot_b, r, c] for r, c in zip(rows, cols)],
                        [b2_all[i][:, c] for (i, _), c in zip(ids, cols)])

            def state_in(g, update):
                i, h = ids[g]
                if chained:
                    st = state[h]
                    state[h] = update(st)
                else:
                    st = s0_ref[i, h].T
                    so_ref[i, h] = update(st).T
                return st

            def finish(outs):
                ms = [jnp.mean(o * o, axis=-1, keepdims=True) for o in outs]
                for g, (r, c) in enumerate(zip(rows, cols)):
                    o = outs[g] * lax.rsqrt(ms[g] + EPS) * og_ref[:, c]
                    mix_s[r, c] = (o * gate_s[slot_b, r, c]).astype(BF16)

            return _hgrn_group(load, state_in, finish, diag_masks, level_masks)

        def b_tail():
            tc = lax.broadcasted_iota(jnp.int32, (sgu_c, sgu_c), 0)
            sc = lax.broadcasted_iota(jnp.int32, (sgu_c, sgu_c), 1)
            for h in range(N_HEADS):
                cs = slice(h * HEAD, (h + 1) * HEAD)
                ws = jnp.where(tc >= sc, ws_ref[h, :sgu_c, :sgu_c], 0.0).astype(BF16)
                for j in range(tm // sgu_c):
                    rows = slice(j * sgu_c, (j + 1) * sgu_c)
                    mixed = _dot(ws, vn_s[slot_b, rows, cs]) + bs_ref[:sgu_c, cs]
                    mix_s[rows, D_HGRN + h * HEAD:D_HGRN + (h + 1) * HEAD] = (
                        u_s[slot_b, rows, cs] * mixed).astype(BF16)
                yield
            xo_ref[...] = xb_ref[...] + _dot(mix_s[...], wout_ref[...])
            yield

        pending = iter(a_pieces)
        segments = [b_hgrn(i0) for i0 in range(0, tm // BLOCK, GROUP_BLOCKS)] + [b_tail()]
        for segment in segments:
            for _ in segment:
                piece = next(pending, None)
                if piece is not None:
                    piece()
        for piece in pending:
            piece()

        if chained:
            for h in range(N_HEADS):
                st_s[h] = state[h]
                so_ref[0, h] = state[h].T

    @pl.when(step % 2 == 0)
    def _():
        both_stages(0, 1)

    @pl.when(step % 2 == 1)
    def _():
        both_stages(1, 0)


def _mixer(x, seq_len, s0, g, w_in, lb_logits, out_gain, ln_g, ln_b, w_s, bias, w_out, layer):
    t = x.shape[0]
    n_seq = t // seq_len
    tm = min(TOKEN_TILE, t)
    assert t % tm == 0 and (tm // BLOCK) % GROUP_BLOCKS == 0
    n_tiles = t // tm
    chained = s0 is None
    if chained:
        assert seq_len % tm == 0 and tm % SGU_CHUNK == 0
        tiles_per_seq, seq_per_tile, sgu_c = seq_len // tm, 1, SGU_CHUNK
    else:
        assert seq_len == BLOCK and tm % BLOCK == 0
        tiles_per_seq, seq_per_tile, sgu_c = 1, tm // BLOCK, BLOCK
    lconst3 = lambda i: (layer, 0, 0)
    tile_a = lambda i: jnp.minimum(i, n_tiles - 1)
    tile_b = lambda i: jnp.maximum(i - 1, 0)
    state_spec = pl.BlockSpec((seq_per_tile, N_HEADS, HEAD, HEAD),
                              lambda i: (tile_b(i) // tiles_per_seq, 0, 0, 0))
    in_specs = [
        pl.BlockSpec((tm, D_MODEL), lambda i: (tile_a(i), 0)),
        pl.BlockSpec((tm, D_MODEL), lambda i: (tile_b(i), 0)),
        pl.BlockSpec((None, 1, D_MODEL), lconst3),
        pl.BlockSpec((None, D_MODEL, D_IN), lconst3),
        pl.BlockSpec(lb_logits.shape, lambda i: (0, 0)),
        pl.BlockSpec((None, 1, D_HGRN), lconst3),
        pl.BlockSpec((None, 1, D_SGU), lconst3),
        pl.BlockSpec((None, 1, D_SGU), lconst3),
        pl.BlockSpec((None, N_HEADS, SGU_CHUNK, SGU_CHUNK), lambda i: (layer, 0, 0, 0)),
        pl.BlockSpec((None, SGU_CHUNK, D_SGU), lconst3),
        pl.BlockSpec((None, D_MODEL, D_MODEL), lconst3),
    ]
    args = [x, x, g, w_in, lb_logits, out_gain, ln_g, ln_b, w_s, bias, w_out]
    out_specs = [pl.BlockSpec((tm, D_MODEL), lambda i: (tile_b(i), 0)), state_spec]
    out_shape = [jax.ShapeDtypeStruct((t, D_MODEL), F32),
                 jax.ShapeDtypeStruct((n_seq, N_HEADS, HEAD, HEAD), F32)]
    if not chained:
        in_specs.append(state_spec)
        args.append(s0)
        out_specs.append(pl.BlockSpec((tm, D_SGU), lambda i: (tile_a(i), 0)))
        out_shape.append(jax.ShapeDtypeStruct((t, D_SGU), F32))
    wide = lambda dt: pltpu.VMEM((2, tm, D_HGRN), dt)
    scratch = [wide(F32), wide(F32), wide(F32), wide(F32), wide(F32), wide(F32), wide(BF16),
               pltpu.VMEM((tm, D_MODEL), BF16), pltpu.VMEM((N_HEADS, HEAD, HEAD), F32)]
    return pl.pallas_call(
        functools.partial(_mixer_kernel, layer=layer, tm=tm, n_tiles=n_tiles, chained=chained,
                          tiles_per_seq=tiles_per_seq, sgu_c=sgu_c),
        grid=(n_tiles + 1,),
        in_specs=in_specs,
        out_specs=out_specs,
        out_shape=out_shape,
        scratch_shapes=scratch,
        compiler_params=pltpu.CompilerParams(
            dimension_semantics=("arbitrary",), vmem_limit_bytes=VMEM_LIMIT_BYTES),
        name="mixer_chained" if chained else "mixer_stepped",
    )(*args)


def _trunk(x3, s0_stack, p):
    bsz, seq_len, _ = x3.shape
    x = x3.reshape(bsz * seq_len, D_MODEL)
    depth = p["w_in"].shape[0]
    states, vrows = [], []
    for l in range(depth):
        x = _ffn(x, p["norm_ffn1"], p["ffn1_w1"], p["ffn1_w3"], p["ffn1_w2"], l)
        s0 = None if s0_stack is None else s0_stack[l]
        res = _mixer(x, seq_len, s0, p["norm_mix"], p["w_in"], p["lb_logits"], p["hgrn_out_norm"],
                     p["sgu_ln_g"], p["sgu_ln_b"], p["sgu_w_s"], p["sgu_bias"], p["w_out"], l)
        x = res[0]
        states.append(res[1])
        if s0 is not None:
            vrows.append(res[2].reshape(bsz, seq_len, D_SGU))
        x = _ffn(x, p["norm_ffn2"], p["ffn2_w1"], p["ffn2_w3"], p["ffn2_w2"], l,
                 final_g=p["final_norm"] if l == depth - 1 else None)
    y = x.reshape(bsz, seq_len, D_MODEL)
    return y, jnp.stack(states), (jnp.stack(vrows) if vrows else None)


def kernel(x_prompt, x_sample, state_hgrn, lb_logits, norm_ffn1, ffn1_w1, ffn1_w3, ffn1_w2, norm_mix, w_in, hgrn_out_norm, sgu_ln_g, sgu_ln_b, sgu_w_s, sgu_b_s, w_out, norm_ffn2, ffn2_w1, ffn2_w3, ffn2_w2, final_norm):
    row = lambda a: a.reshape(a.shape[0], 1, a.shape[1])
    p = dict(
        lb_logits=lb_logits,
        norm_ffn1=row(norm_ffn1), norm_mix=row(norm_mix), norm_ffn2=row(norm_ffn2),
        hgrn_out_norm=row(hgrn_out_norm), sgu_ln_g=row(sgu_ln_g), sgu_ln_b=row(sgu_ln_b),
        final_norm=final_norm.reshape(1, D_MODEL),
        ffn1_w1=ffn1_w1.astype(BF16), ffn1_w3=ffn1_w3.astype(BF16), ffn1_w2=ffn1_w2.astype(BF16),
        ffn2_w1=ffn2_w1.astype(BF16), ffn2_w3=ffn2_w3.astype(BF16), ffn2_w2=ffn2_w2.astype(BF16),
        w_in=w_in.astype(BF16), w_out=w_out.astype(BF16),
        sgu_w_s=sgu_w_s,
        sgu_bias=jnp.repeat(jnp.swapaxes(sgu_b_s, 1, 2), HEAD, axis=2),
    )
    y_prompt, s_prompt, _ = _trunk(x_prompt, None, p)
    y_sample, s_sample, v_sample = _trunk(x_sample, state_hgrn, p)
    return (y_prompt, y_sample, s_prompt, s_sample, v_sample)
```

```python
import functools

import jax
import jax.numpy as jnp
from jax import lax
from jax.experimental import pallas as pl
from jax.experimental.pallas import tpu as pltpu

F32 = jnp.float32
BF16 = jnp.bfloat16

D_MODEL = 1024
D_FF = 2816
D_HGRN = 512
D_SGU = 512
N_HEADS = 4
HEAD = 128
D_IN = 4 * D_HGRN + 2 * D_SGU
BLOCK = 64
SUBLANES = 8
SUB = 4
GROUP_BLOCKS = 2
SGU_CHUNK = 128
EPS = 1e-6
FFN_HALF = 0.5

TOKEN_TILE = 512
FFN_PARTS = 2
FF_CHUNK = 256
A_COLS = 256
VMEM_LIMIT_BYTES = 56 * 1024 * 1024


def _rms(x, g):
    return x * lax.rsqrt(jnp.mean(x * x, axis=-1, keepdims=True) + EPS) * g


def _dot(a, b):
    return jnp.dot(a, b, preferred_element_type=F32)


def _dot_nt(a, b):
    return lax.dot_general(a, b, (((1,), (1,)), ((), ())), preferred_element_type=F32)


def _dot_tn(a, b):
    return lax.dot_general(a, b, (((0,), (0,)), ((), ())), preferred_element_type=F32)


def _ffn_kernel(*refs, final):
    if final:
        x_ref, g_ref, w1_ref, w3_ref, w2_ref, fg_ref, o_ref, h_s = refs
    else:
        x_ref, g_ref, w1_ref, w3_ref, w2_ref, o_ref, h_s = refs
    part_rows = x_ref.shape[0] // FFN_PARTS
    for part in range(FFN_PARTS):
        rows = slice(part * part_rows, (part + 1) * part_rows)
        x = x_ref[rows, :]
        xn = _rms(x, g_ref[...]).astype(BF16)
        for c in range(D_FF // FF_CHUNK):
            cs = slice(c * FF_CHUNK, (c + 1) * FF_CHUNK)
            a = _dot(xn, w1_ref[:, cs])
            b = _dot(xn, w3_ref[:, cs])
            h_s[rows, cs] = (jax.nn.silu(a) * b).astype(BF16)
        out = x + FFN_HALF * _dot(h_s[rows, :], w2_ref[...])
        if final:
            out = _rms(out, fg_ref[...])
        o_ref[rows, :] = out


def _ffn(x, g, w1, w3, w2, layer, final_g=None):
    t = x.shape[0]
    tm = min(FFN_PARTS * TOKEN_TILE, t)
    assert t % tm == 0 and tm % FFN_PARTS == 0
    final = final_g is not None
    const = lambda i: (layer, 0, 0)
    resident = dict(pipeline_mode=pl.Buffered(1))
    in_specs = [
        pl.BlockSpec((tm, D_MODEL), lambda i: (i, 0)),
        pl.BlockSpec((None, 1, D_MODEL), const),
        pl.BlockSpec((None, D_MODEL, D_FF), const, **resident),
        pl.BlockSpec((None, D_MODEL, D_FF), const, **resident),
        pl.BlockSpec((None, D_FF, D_MODEL), const, **resident),
    ]
    args = [x, g, w1, w3, w2]
    if final:
        in_specs.append(pl.BlockSpec((1, D_MODEL), lambda i: (0, 0)))
        args.append(final_g)
    return pl.pallas_call(
        functools.partial(_ffn_kernel, final=final),
        grid=(t // tm,),
        in_specs=in_specs,
        out_specs=pl.BlockSpec((tm, D_MODEL), lambda i: (i, 0)),
        out_shape=jax.ShapeDtypeStruct((t, D_MODEL), F32),
        scratch_shapes=[pltpu.VMEM((tm, D_FF), BF16)],
        compiler_params=pltpu.CompilerParams(
            dimension_semantics=("arbitrary",), vmem_limit_bytes=VMEM_LIMIT_BYTES),
        name="ffn_final" if final else "ffn",
    )(*args)


def _roll_rows(x, d):
    n, c = x.shape
    return pltpu.roll(x.reshape(n // SUBLANES, SUBLANES, c), d, 1).reshape(n, c)


def _level_operand(q, k, b2, half):
    if half >= SUBLANES:
        pieces = []
        for r0 in range(0, BLOCK, half):
            mid = (r0 // (2 * half)) * 2 * half + half - 1
            if (r0 // half) % 2 == 1:
                pieces.append(q[r0:r0 + half] * jnp.exp2(b2[r0:r0 + half] - b2[mid:mid + 1]))
            else:
                pieces.append(k[r0:r0 + half] * jnp.exp2(b2[mid:mid + 1] - b2[r0:r0 + half]))
        return jnp.concatenate(pieces, axis=0).astype(BF16)
    assert 2 * half == SUBLANES
    b3 = b2.reshape(BLOCK // SUBLANES, SUBLANES, HEAD)
    ref = jnp.broadcast_to(b3[:, half - 1:half, :], b3.shape).reshape(BLOCK, HEAD)
    upper = lax.broadcasted_iota(jnp.int32, (BLOCK, HEAD), 0) % SUBLANES >= half
    return (jnp.where(upper, q, k) * jnp.exp2(-jnp.abs(b2 - ref))).astype(BF16)


def _hgrn_group(load, state_in, finish, diag_masks, level_masks):
    qs, ks, vs, b2s = load()
    n = len(qs)
    chains = range(n)
    vbs = [v.astype(BF16) for v in vs]
    near = [[jnp.sum(qs[g] * ks[g], axis=-1, keepdims=True) for g in chains]]
    fs = [1.0 - k for k in ks]
    es = fs
    for d in range(1, SUB):
        if d > 1:
            es = [es[g] * _roll_rows(fs[g], d - 1) for g in chains]
        near.append([jnp.sum(qs[g] * _roll_rows(ks[g], d) * es[g], axis=-1, keepdims=True)
                     for g in chains])
    yield
    far = []
    for half, _ in level_masks:
        ws = [_level_operand(qs[g], ks[g], b2s[g], half) for g in chains]
        far.append([_dot_nt(w, w) for w in ws])
    yield
    b_last = [b2[BLOCK - 1:BLOCK] for b2 in b2s]
    kdec = [(ks[g] * jnp.exp2(b_last[g] - b2s[g])).astype(BF16) for g in chains]
    grow = [_dot_tn(vbs[g], kdec[g]) for g in chains]
    st_in = [state_in(g, lambda st, g=g: st * jnp.exp2(b_last[g]) + grow[g]) for g in chains]
    qdec = [(qs[g] * jnp.exp2(b2s[g])).astype(BF16) for g in chains]
    o = [_dot_nt(qdec[g], st_in[g].astype(BF16)) for g in chains]
    yield
    att = [jnp.where(diag_masks[0], near[0][g], 0.0) for g in chains]
    for d in range(1, SUB):
        att = [jnp.where(diag_masks[d], near[d][g], att[g]) for g in chains]
    for lvl, (_, mask) in enumerate(level_masks):
        att = [jnp.where(mask, far[lvl][g], att[g]) for g in chains]
    finish([o[g] + _dot(att[g].astype(BF16), vbs[g]) for g in chains])
    yield


def _mixer_kernel(*refs, layer, tm, n_tiles, chained, tiles_per_seq, sgu_c):
    if chained:
        (xa_ref, xb_ref, g_ref, win_ref, lbl_ref, og_ref, lng_ref, lnb_ref, ws_ref, bs_ref, wout_ref,
         xo_ref, so_ref,
         q_s, k_s, v_s, lf_s, gate_s, u_s, vn_s, mix_s, st_s) = refs
        s0_ref = vno_ref = None
    else:
        (xa_ref, xb_ref, g_ref, win_ref, lbl_ref, og_ref, lng_ref, lnb_ref, ws_ref, bs_ref, wout_ref,
         s0_ref,
         xo_ref, so_ref, vno_ref,
         q_s, k_s, v_s, lf_s, gate_s, u_s, vn_s, mix_s, st_s) = refs
    step = pl.program_id(0)
    tile_b = jnp.maximum(step - 1, 0)

    @pl.when(step == 0)
    def _():
        for ref in (q_s, k_s, v_s, lf_s, gate_s, u_s, vn_s):
            ref[1] = jnp.zeros(ref.shape[1:], ref.dtype)

    if chained:
        @pl.when(tile_b % tiles_per_seq == 0)
        def _():
            st_s[...] = jnp.zeros_like(st_s)

    def both_stages(slot_a, slot_b):
        state = [st_s[h] for h in range(N_HEADS)] if chained else None

        hn = _rms(xa_ref[...], g_ref[...]).astype(BF16)

        lg = lbl_ref[...]
        e = jnp.exp(lg - jnp.max(lg, axis=0, keepdims=True))
        p = e / jnp.sum(e, axis=0, keepdims=True)
        lb = p[0:1]
        for j in range(1, layer + 1):
            lb = lb + p[j:j + 1]
        lb = lb - p[0:1]

        def a_piece(group, half):
            cs = slice(half * A_COLS, (half + 1) * A_COLS)
            z = _dot(hn, win_ref[:, group * D_HGRN + half * A_COLS:group * D_HGRN + (half + 1) * A_COLS])
            if group == 0:
                q_s[slot_a, :, cs] = jax.nn.silu(z) * (HEAD ** -0.5)
            elif group == 1:
                f = lb[:, cs] + (1.0 - lb[:, cs]) * jax.nn.sigmoid(z)
                lf_s[slot_a, :, cs] = jnp.log2(f)
                k_s[slot_a, :, cs] = 1.0 - f
            elif group == 2:
                v_s[slot_a, :, cs] = z
            elif group == 3:
                gate_s[slot_a, :, cs] = jax.nn.silu(z)
            elif group == 4:
                u_s[slot_a, :, cs] = jax.nn.gelu(z)
            else:
                vg = jax.nn.gelu(z)
                for h in range(A_COLS // HEAD):
                    hs = slice(h * HEAD, (h + 1) * HEAD)
                    os_ = slice(half * A_COLS + h * HEAD, half * A_COLS + (h + 1) * HEAD)
                    vh = vg[:, hs]
                    mu = jnp.mean(vh, axis=-1, keepdims=True)
                    var = jnp.mean(jnp.square(vh - mu), axis=-1, keepdims=True)
                    vn = (vh - mu) * lax.rsqrt(var + EPS) * lng_ref[:, os_] + lnb_ref[:, os_]
                    if vno_ref is not None:
                        vno_ref[:, os_] = vn
                    vn_s[slot_a, :, os_] = vn.astype(BF16)

        a_pieces = [functools.partial(a_piece, group, half)
                    for group in range(D_IN // D_HGRN) for half in range(D_HGRN // A_COLS)]

        ti = lax.broadcasted_iota(jnp.int32, (BLOCK, BLOCK), 0)
        si = lax.broadcasted_iota(jnp.int32, (BLOCK, BLOCK), 1)
        tril3 = jnp.concatenate([(ti >= si).astype(BF16)] * 3, axis=1)
        diag_masks = [ti - si == d for d in range(SUB)]
        level_masks = []
        half = BLOCK // 2
        while half >= SUB:
            same = (ti // (2 * half)) == (si // (2 * half))
            level_masks.append((half, same & ((ti % (2 * half)) >= half) & ((si % (2 * half)) < half)))
            half //= 2

        def cum_log2_decay(i):
            lf = lf_s[slot_b, i * BLOCK:(i + 1) * BLOCK, :]
            lf_hi = lf.astype(BF16)
            rem = lf - lf_hi.astype(F32)
            lf_mid = rem.astype(BF16)
            lf_lo = (rem - lf_mid.astype(F32)).astype(BF16)
            return _dot(tril3, jnp.concatenate([lf_hi, lf_mid, lf_lo], axis=0))

        def b_hgrn(i0):
            blocks = range(i0, i0 + GROUP_BLOCKS)
            ids = [(i, h) for i in blocks for h in range(N_HEADS)]
            rows = [slice(i * BLOCK, (i + 1) * BLOCK) for i, _ in ids]
            cols = [slice(h * HEAD, (h + 1) * HEAD) for _, h in ids]

            def load():
                b2_all = {i: cum_log2_decay(i) for i in blocks}
                return ([q_s[slot_b, r, c] for r, c in zip(rows, cols)],
                        [k_s[slot_b, r, c] for r, c in zip(rows, cols)],
                        [v_s[slot_b, r, c] for r, c in zip(rows, cols)],
                        [b2_all[i][:, c] for (i, _), c in zip(ids, cols)])

            def state_in(g, update):
                i, h = ids[g]
                if chained:
                    st = state[h]
                    state[h] = update(st)
                else:
                    st = s0_ref[i, h].T
                    so_ref[i, h] = update(st).T
                return st

            def finish(outs):
                ms = [jnp.mean(o * o, axis=-1, keepdims=True) for o in outs]
                for g, (r, c) in enumerate(zip(rows, cols)):
                    o = outs[g] * lax.rsqrt(ms[g] + EPS) * og_ref[:, c]
                    mix_s[r, c] = (o * gate_s[slot_b, r, c]).astype(BF16)

            return _hgrn_group(load, state_in, finish, diag_masks, level_masks)

        def b_tail():
            tc = lax.broadcasted_iota(jnp.int32, (sgu_c, sgu_c), 0)
            sc = lax.broadcasted_iota(jnp.int32, (sgu_c, sgu_c), 1)
            for h in range(N_HEADS):
                cs = slice(h * HEAD, (h + 1) * HEAD)
                ws = jnp.where(tc >= sc, ws_ref[h, :sgu_c, :sgu_c], 0.0).astype(BF16)
                for j in range(tm // sgu_c):
                    rows = slice(j * sgu_c, (j + 1) * sgu_c)
                    mixed = _dot(ws, vn_s[slot_b, rows, cs]) + bs_ref[:sgu_c, cs]
                    mix_s[rows, D_HGRN + h * HEAD:D_HGRN + (h + 1) * HEAD] = (
                        u_s[slot_b, rows, cs] * mixed).astype(BF16)
                yield
            xo_ref[...] = xb_ref[...] + _dot(mix_s[...], wout_ref[...])
            yield

        pending = iter(a_pieces)
        segments = [b_hgrn(i0) for i0 in range(0, tm // BLOCK, GROUP_BLOCKS)] + [b_tail()]
        for segment in segments:
            for _ in segment:
                piece = next(pending, None)
                if piece is not None:
                    piece()
        for piece in pending:
            piece()

        if chained:
            for h in range(N_HEADS):
                st_s[h] = state[h]
                so_ref[0, h] = state[h].T

    @pl.when(step % 2 == 0)
    def _():
        both_stages(0, 1)

    @pl.when(step % 2 == 1)
    def _():
        both_stages(1, 0)


def _mixer(x, seq_len, s0, g, w_in, lb_logits, out_gain, ln_g, ln_b, w_s, bias, w_out, layer):
    t = x.shape[0]
    n_seq = t // seq_len
    tm = min(TOKEN_TILE, t)
    assert t % tm == 0 and (tm // BLOCK) % GROUP_BLOCKS == 0
    n_tiles = t // tm
    chained = s0 is None
    if chained:
        assert seq_len % tm == 0 and tm % SGU_CHUNK == 0
        tiles_per_seq, seq_per_tile, sgu_c = seq_len // tm, 1, SGU_CHUNK
    else:
        assert seq_len == BLOCK and tm % BLOCK == 0
        tiles_per_seq, seq_per_tile, sgu_c = 1, tm // BLOCK, BLOCK
    lconst3 = lambda i: (layer, 0, 0)
    tile_a = lambda i: jnp.minimum(i, n_tiles - 1)
    tile_b = lambda i: jnp.maximum(i - 1, 0)
    state_spec = pl.BlockSpec((seq_per_tile, N_HEADS, HEAD, HEAD),
                              lambda i: (tile_b(i) // tiles_per_seq, 0, 0, 0))
    in_specs = [
        pl.BlockSpec((tm, D_MODEL), lambda i: (tile_a(i), 0)),
        pl.BlockSpec((tm, D_MODEL), lambda i: (tile_b(i), 0)),
        pl.BlockSpec((None, 1, D_MODEL), lconst3),
        pl.BlockSpec((None, D_MODEL, D_IN), lconst3),
        pl.BlockSpec(lb_logits.shape, lambda i: (0, 0)),
        pl.BlockSpec((None, 1, D_HGRN), lconst3),
        pl.BlockSpec((None, 1, D_SGU), lconst3),
        pl.BlockSpec((None, 1, D_SGU), lconst3),
        pl.BlockSpec((None, N_HEADS, SGU_CHUNK, SGU_CHUNK), lambda i: (layer, 0, 0, 0)),
        pl.BlockSpec((None, SGU_CHUNK, D_SGU), lconst3),
        pl.BlockSpec((None, D_MODEL, D_MODEL), lconst3),
    ]
    args = [x, x, g, w_in, lb_logits, out_gain, ln_g, ln_b, w_s, bias, w_out]
    out_specs = [pl.BlockSpec((tm, D_MODEL), lambda i: (tile_b(i), 0)), state_spec]
    out_shape = [jax.ShapeDtypeStruct((t, D_MODEL), F32),
                 jax.ShapeDtypeStruct((n_seq, N_HEADS, HEAD, HEAD), F32)]
    if not chained:
        in_specs.append(state_spec)
        args.append(s0)
        out_specs.append(pl.BlockSpec((tm, D_SGU), lambda i: (tile_a(i), 0)))
        out_shape.append(jax.ShapeDtypeStruct((t, D_SGU), F32))
    wide = lambda dt: pltpu.VMEM((2, tm, D_HGRN), dt)
    scratch = [wide(F32), wide(F32), wide(F32), wide(F32), wide(F32), wide(F32), wide(BF16),
               pltpu.VMEM((tm, D_MODEL), BF16), pltpu.VMEM((N_HEADS, HEAD, HEAD), F32)]
    return pl.pallas_call(
        functools.partial(_mixer_kernel, layer=layer, tm=tm, n_tiles=n_tiles, chained=chained,
                          tiles_per_seq=tiles_per_seq, sgu_c=sgu_c),
        grid=(n_tiles + 1,),
        in_specs=in_specs,
        out_specs=out_specs,
        out_shape=out_shape,
        scratch_shapes=scratch,
        compiler_params=pltpu.CompilerParams(
            dimension_semantics=("arbitrary",), vmem_limit_bytes=VMEM_LIMIT_BYTES),
        name="mixer_chained" if chained else "mixer_stepped",
    )(*args)


def _trunk(x3, s0_stack, p):
    bsz, seq_len, _ = x3.shape
    x = x3.reshape(bsz * seq_len, D_MODEL)
    depth = p["w_in"].shape[0]
    states, vrows = [], []
    for l in range(depth):
        x = _ffn(x, p["norm_ffn1"], p["ffn1_w1"], p["ffn1_w3"], p["ffn1_w2"], l)
        s0 = None if s0_stack is None else s0_stack[l]
        res = _mixer(x, seq_len, s0, p["norm_mix"], p["w_in"], p["lb_logits"], p["hgrn_out_norm"],
                     p["sgu_ln_g"], p["sgu_ln_b"], p["sgu_w_s"], p["sgu_bias"], p["w_out"], l)
        x = res[0]
        states.append(res[1])
        if s0 is not None:
            vrows.append(res[2].reshape(bsz, seq_len, D_SGU))
        x = _ffn(x, p["norm_ffn2"], p["ffn2_w1"], p["ffn2_w3"], p["ffn2_w2"], l,
                 final_g=p["final_norm"] if l == depth - 1 else None)
    y = x.reshape(bsz, seq_len, D_MODEL)
    return y, jnp.stack(states), (jnp.stack(vrows) if vrows else None)


def kernel(x_prompt, x_sample, state_hgrn, lb_logits, norm_ffn1, ffn1_w1, ffn1_w3, ffn1_w2, norm_mix, w_in, hgrn_out_norm, sgu_ln_g, sgu_ln_b, sgu_w_s, sgu_b_s, w_out, norm_ffn2, ffn2_w1, ffn2_w3, ffn2_w2, final_norm):
    row = lambda a: a.reshape(a.shape[0], 1, a.shape[1])
    p = dict(
        lb_logits=lb_logits,
        norm_ffn1=row(norm_ffn1), norm_mix=row(norm_mix), norm_ffn2=row(norm_ffn2),
        hgrn_out_norm=row(hgrn_out_norm), sgu_ln_g=row(sgu_ln_g), sgu_ln_b=row(sgu_ln_b),
        final_norm=final_norm.reshape(1, D_MODEL),
        ffn1_w1=ffn1_w1.astype(BF16), ffn1_w3=ffn1_w3.astype(BF16), ffn1_w2=ffn1_w2.astype(BF16),
        ffn2_w1=ffn2_w1.astype(BF16), ffn2_w3=ffn2_w3.astype(BF16), ffn2_w2=ffn2_w2.astype(BF16),
        w_in=w_in.astype(BF16), w_out=w_out.astype(BF16),
        sgu_w_s=sgu_w_s,
        sgu_bias=jnp.repeat(jnp.swapaxes(sgu_b_s, 1, 2), HEAD, axis=2),
    )
    y_prompt, s_prompt, _ = _trunk(x_prompt, None, p)
    y_sample, s_sample, v_sample = _trunk(x_sample, state_hgrn, p)
    return (y_prompt, y_sample, s_prompt, s_sample, v_sample)
```

```python
import functools

import jax
import jax.numpy as jnp
from jax import lax
from jax.experimental import pallas as pl
from jax.experimental.pallas import tpu as pltpu

F32 = jnp.float32
BF16 = jnp.bfloat16

D_MODEL = 1024
D_FF = 2816
D_HGRN = 512
D_SGU = 512
N_HEADS = 4
HEAD = 128
D_IN = 4 * D_HGRN + 2 * D_SGU
BLOCK = 64
SUBLANES = 8
SUB = 4
GROUP_BLOCKS = 2
SGU_CHUNK = 128
EPS = 1e-6
FFN_HALF = 0.5

TOKEN_TILE = 512
FFN_PARTS = 2
FF_CHUNK = 256
A_COLS = 256
VMEM_LIMIT_BYTES = 56 * 1024 * 1024


def _rms(x, g):
    return x * lax.rsqrt(jnp.mean(x * x, axis=-1, keepdims=True) + EPS) * g


def _dot(a, b):
    return jnp.dot(a, b, preferred_element_type=F32)


def _dot_nt(a, b):
    return lax.dot_general(a, b, (((1,), (1,)), ((), ())), preferred_element_type=F32)


def _dot_tn(a, b):
    return lax.dot_general(a, b, (((0,), (0,)), ((), ())), preferred_element_type=F32)


def _ffn_kernel(*refs, final):
    if final:
        x_ref, g_ref, w1_ref, w3_ref, w2_ref, fg_ref, o_ref, h_s = refs
    else:
        x_ref, g_ref, w1_ref, w3_ref, w2_ref, o_ref, h_s = refs
    part_rows = x_ref.shape[0] // FFN_PARTS
    for part in range(FFN_PARTS):
        rows = slice(part * part_rows, (part + 1) * part_rows)
        x = x_ref[rows, :]
        xn = _rms(x, g_ref[...]).astype(BF16)
        for c in range(D_FF // FF_CHUNK):
            cs = slice(c * FF_CHUNK, (c + 1) * FF_CHUNK)
            a = _dot(xn, w1_ref[:, cs])
            b = _dot(xn, w3_ref[:, cs])
            h_s[rows, cs] = (jax.nn.silu(a) * b).astype(BF16)
        out = x + FFN_HALF * _dot(h_s[rows, :], w2_ref[...])
        if final:
            out = _rms(out, fg_ref[...])
        o_ref[rows, :] = out


def _ffn(x, g, w1, w3, w2, layer, final_g=None):
    t = x.shape[0]
    tm = min(FFN_PARTS * TOKEN_TILE, t)
    assert t % tm == 0 and tm % FFN_PARTS == 0
    final = final_g is not None
    const = lambda i: (layer, 0, 0)
    resident = dict(pipeline_mode=pl.Buffered(1))
    in_specs = [
        pl.BlockSpec((tm, D_MODEL), lambda i: (i, 0)),
        pl.BlockSpec((None, 1, D_MODEL), const),
        pl.BlockSpec((None, D_MODEL, D_FF), const, **resident),
        pl.BlockSpec((None, D_MODEL, D_FF), const, **resident),
        pl.BlockSpec((None, D_FF, D_MODEL), const, **resident),
    ]
    args = [x, g, w1, w3, w2]
    if final:
        in_specs.append(pl.BlockSpec((1, D_MODEL), lambda i: (0, 0)))
        args.append(final_g)
    return pl.pallas_call(
        functools.partial(_ffn_kernel, final=final),
        grid=(t // tm,),
        in_specs=in_specs,
        out_specs=pl.BlockSpec((tm, D_MODEL), lambda i: (i, 0)),
        out_shape=jax.ShapeDtypeStruct((t, D_MODEL), F32),
        scratch_shapes=[pltpu.VMEM((tm, D_FF), BF16)],
        compiler_params=pltpu.CompilerParams(
            dimension_semantics=("arbitrary",), vmem_limit_bytes=VMEM_LIMIT_BYTES),
        name="ffn_final" if final else "ffn",
    )(*args)


def _roll_rows(x, d):
    n, c = x.shape
    return pltpu.roll(x.reshape(n // SUBLANES, SUBLANES, c), d, 1).reshape(n, c)


def _side_by_side(a, b):
    return jnp.concatenate([a, b], axis=1)


def _block_diag(a, b):
    return jnp.concatenate([_side_by_side(a, jnp.zeros_like(b)),
                            _side_by_side(jnp.zeros_like(a), b)], axis=0)


def _level_operand(q, k, b2, half):
    if half >= SUBLANES:
        pieces = []
        for r0 in range(0, BLOCK, half):
            mid = (r0 // (2 * half)) * 2 * half + half - 1
            if (r0 // half) % 2 == 1:
                pieces.append(q[r0:r0 + half] * jnp.exp2(b2[r0:r0 + half] - b2[mid:mid + 1]))
            else:
                pieces.append(k[r0:r0 + half] * jnp.exp2(b2[mid:mid + 1] - b2[r0:r0 + half]))
        return jnp.concatenate(pieces, axis=0).astype(BF16)
    assert 2 * half == SUBLANES
    b3 = b2.reshape(BLOCK // SUBLANES, SUBLANES, HEAD)
    ref = jnp.broadcast_to(b3[:, half - 1:half, :], b3.shape).reshape(BLOCK, HEAD)
    upper = lax.broadcasted_iota(jnp.int32, (BLOCK, HEAD), 0) % SUBLANES >= half
    return (jnp.where(upper, q, k) * jnp.exp2(-jnp.abs(b2 - ref))).astype(BF16)


def _hgrn_group(load, state_in, finish, diag_masks, level_masks):
    qs, ks, vs, b2s = load()
    n = len(qs)
    assert n % 2 == 0
    chains = range(n)
    pairs = [(g, g + 1) for g in range(0, n, 2)]
    vbs = [v.astype(BF16) for v in vs]

    near = [[jnp.sum(qs[g] * ks[g], axis=-1, keepdims=True) for g in chains]]
    fs = [1.0 - k for k in ks]
    es = fs
    for d in range(1, SUB):
        if d > 1:
            es = [es[g] * _roll_rows(fs[g], d - 1) for g in chains]
        near.append([jnp.sum(qs[g] * _roll_rows(ks[g], d) * es[g], axis=-1, keepdims=True)
                     for g in chains])
    level_ws = [[_level_operand(qs[g], ks[g], b2s[g], half) for g in chains] for half, _ in level_masks]
    b_last = [b2[BLOCK - 1:BLOCK] for b2 in b2s]
    kdec = [(ks[g] * jnp.exp2(b_last[g] - b2s[g])).astype(BF16) for g in chains]
    qdec = [(qs[g] * jnp.exp2(b2s[g])).astype(BF16) for g in chains]
    yield
    far = [[_dot_nt(_side_by_side(ws[a], ws[b]), _block_diag(ws[a], ws[b])) for a, b in pairs]
           for ws in level_ws]
    grow = [_dot_tn(vbs[g], kdec[g]) for g in chains]
    yield
    st_in = [state_in(g, lambda st, g=g: st * jnp.exp2(b_last[g]) + grow[g]).astype(BF16) for g in chains]
    att = [jnp.where(diag_masks[0][0], near[0][a], jnp.where(diag_masks[0][1], near[0][b], 0.0))
           for a, b in pairs]
    for d in range(1, SUB):
        att = [jnp.where(diag_masks[d][0], near[d][a], jnp.where(diag_masks[d][1], near[d][b], att[p]))
               for p, (a, b) in enumerate(pairs)]
    for lvl, (_, mask) in enumerate(level_masks):
        att = [jnp.where(mask, far[lvl][p], att[p]) for p in range(len(pairs))]
    att = [a.astype(BF16) for a in att]
    yield
    o_state = [_dot_nt(_side_by_side(qdec[a], qdec[b]), _block_diag(st_in[a], st_in[b])) for a, b in pairs]
    o_block = [_dot(att[p], _block_diag(vbs[a], vbs[b])) for p, (a, b) in enumerate(pairs)]
    yield
    outs = []
    for p in range(len(pairs)):
        o_pair = o_state[p] + o_block[p]
        outs += [o_pair[:, :HEAD], o_pair[:, HEAD:]]
    finish(outs)
    yield


def _mixer_kernel(*refs, layer, tm, n_tiles, chained, tiles_per_seq, sgu_c):
    if chained:
        (xa_ref, xb_ref, g_ref, win_ref, lbl_ref, og_ref, lng_ref, lnb_ref, ws_ref, bs_ref, wout_ref,
         xo_ref, so_ref,
         q_s, k_s, v_s, lf_s, gate_s, u_s, vn_s, mix_s, st_s) = refs
        s0_ref = vno_ref = None
    else:
        (xa_ref, xb_ref, g_ref, win_ref, lbl_ref, og_ref, lng_ref, lnb_ref, ws_ref, bs_ref, wout_ref,
         s0_ref,
         xo_ref, so_ref, vno_ref,
         q_s, k_s, v_s, lf_s, gate_s, u_s, vn_s, mix_s, st_s) = refs
    step = pl.program_id(0)
    tile_b = jnp.maximum(step - 1, 0)

    @pl.when(step == 0)
    def _():
        for ref in (q_s, k_s, v_s, lf_s, gate_s, u_s, vn_s):
            ref[1] = jnp.zeros(ref.shape[1:], ref.dtype)

    if chained:
        @pl.when(tile_b % tiles_per_seq == 0)
        def _():
            st_s[...] = jnp.zeros_like(st_s)

    def both_stages(slot_a, slot_b):
        state = [st_s[h] for h in range(N_HEADS)] if chained else None

        hn = _rms(xa_ref[...], g_ref[...]).astype(BF16)

        lg = lbl_ref[...]
        e = jnp.exp(lg - jnp.max(lg, axis=0, keepdims=True))
        p = e / jnp.sum(e, axis=0, keepdims=True)
        lb = p[0:1]
        for j in range(1, layer + 1):
            lb = lb + p[j:j + 1]
        lb = lb - p[0:1]

        def a_piece(group, half):
            cs = slice(half * A_COLS, (half + 1) * A_COLS)
            z = _dot(hn, win_ref[:, group * D_HGRN + half * A_COLS:group * D_HGRN + (half + 1) * A_COLS])
            if group == 0:
                q_s[slot_a, :, cs] = jax.nn.silu(z) * (HEAD ** -0.5)
            elif group == 1:
                f = lb[:, cs] + (1.0 - lb[:, cs]) * jax.nn.sigmoid(z)
                lf_s[slot_a, :, cs] = jnp.log2(f)
                k_s[slot_a, :, cs] = 1.0 - f
            elif group == 2:
                v_s[slot_a, :, cs] = z
            elif group == 3:
                gate_s[slot_a, :, cs] = jax.nn.silu(z)
            elif group == 4:
                u_s[slot_a, :, cs] = jax.nn.gelu(z)
            else:
                vg = jax.nn.gelu(z)
                for h in range(A_COLS // HEAD):
                    hs = slice(h * HEAD, (h + 1) * HEAD)
                    os_ = slice(half * A_COLS + h * HEAD, half * A_COLS + (h + 1) * HEAD)
                    vh = vg[:, hs]
                    mu = jnp.mean(vh, axis=-1, keepdims=True)
                    var = jnp.mean(jnp.square(vh - mu), axis=-1, keepdims=True)
                    vn = (vh - mu) * lax.rsqrt(var + EPS) * lng_ref[:, os_] + lnb_ref[:, os_]
                    if vno_ref is not None:
                        vno_ref[:, os_] = vn
                    vn_s[slot_a, :, os_] = vn.astype(BF16)

        a_pieces = [functools.partial(a_piece, group, half)
                    for group in range(D_IN // D_HGRN) for half in range(D_HGRN // A_COLS)]

        ti = lax.broadcasted_iota(jnp.int32, (BLOCK, BLOCK), 0)
        si = lax.broadcasted_iota(jnp.int32, (BLOCK, BLOCK), 1)
        tril3 = jnp.concatenate([(ti >= si).astype(BF16)] * 3, axis=1)
        ti = lax.broadcasted_iota(jnp.int32, (BLOCK, 2 * BLOCK), 0)
        lane = lax.broadcasted_iota(jnp.int32, (BLOCK, 2 * BLOCK), 1)
        si = lane % BLOCK
        diag_masks = [((ti - si == d) & (lane < BLOCK), (ti - si == d) & (lane >= BLOCK))
                      for d in range(SUB)]
        level_masks = []
        half = BLOCK // 2
        while half >= SUB:
            same = (ti // (2 * half)) == (si // (2 * half))
            level_masks.append((half, same & ((ti % (2 * half)) >= half) & ((si % (2 * half)) < half)))
            half //= 2

        def cum_log2_decay(i):
            lf = lf_s[slot_b, i * BLOCK:(i + 1) * BLOCK, :]
            lf_hi = lf.astype(BF16)
            rem = lf - lf_hi.astype(F32)
            lf_mid = rem.astype(BF16)
            lf_lo = (rem - lf_mid.astype(F32)).astype(BF16)
            return _dot(tril3, jnp.concatenate([lf_hi, lf_mid, lf_lo], axis=0))

        def b_hgrn(i0):
            blocks = range(i0, i0 + GROUP_BLOCKS)
            ids = [(i, h) for i in blocks for h in range(N_HEADS)]
            rows = [slice(i * BLOCK, (i + 1) * BLOCK) for i, _ in ids]
            cols = [slice(h * HEAD, (h + 1) * HEAD) for _, h in ids]

            def load():
                b2_all = {i: cum_log2_decay(i) for i in blocks}
                return ([q_s[slot_b, r, c] for r, c in zip(rows, cols)],
                        [k_s[slot_b, r, c] for r, c in zip(rows, cols)],
                        [v_s[slot_b, r, c] for r, c in zip(rows, cols)],
                        [b2_all[i][:, c] for (i, _), c in zip(ids, cols)])

            def state_in(g, update):
                i, h = ids[g]
                if chained:
                    st = state[h]
                    state[h] = update(st)
                else:
                    st = s0_ref[i, h].T
                    so_ref[i, h] = update(st).T
                return st

            def finish(outs):
                ms = [jnp.mean(o * o, axis=-1, keepdims=True) for o in outs]
                for g, (r, c) in enumerate(zip(rows, cols)):
                    o = outs[g] * lax.rsqrt(ms[g] + EPS) * og_ref[:, c]
                    mix_s[r, c] = (o * gate_s[slot_b, r, c]).astype(BF16)

            return _hgrn_group(load, state_in, finish, diag_masks, level_masks)

        def b_tail():
            tc = lax.broadcasted_iota(jnp.int32, (sgu_c, sgu_c), 0)
            sc = lax.broadcasted_iota(jnp.int32, (sgu_c, sgu_c), 1)
            for h in range(0, N_HEADS, 2):
                ca = slice(h * HEAD, (h + 1) * HEAD)
                cb = slice((h + 1) * HEAD, (h + 2) * HEAD)
                cs = slice(h * HEAD, (h + 2) * HEAD)
                ws = _side_by_side(*[jnp.where(tc >= sc, ws_ref[hh, :sgu_c, :sgu_c], 0.0).astype(BF16)
                                     for hh in (h, h + 1)])
                for j in range(tm // sgu_c):
                    rows = slice(j * sgu_c, (j + 1) * sgu_c)
                    vn_pair = _block_diag(vn_s[slot_b, rows, ca], vn_s[slot_b, rows, cb])
                    mixed = _dot(ws, vn_pair) + bs_ref[:sgu_c, cs]
                    mix_s[rows, D_HGRN + h * HEAD:D_HGRN + (h + 2) * HEAD] = (
                        u_s[slot_b, rows, cs] * mixed).astype(BF16)
                yield
            xo_ref[...] = xb_ref[...] + _dot(mix_s[...], wout_ref[...])
            yield

        pending = iter(a_pieces)
        for i0 in range(0, tm // BLOCK, GROUP_BLOCKS):
            for stage, _ in enumerate(b_hgrn(i0)):
                if stage % 2 == 0:
                    piece = next(pending, None)
                    if piece is not None:
                        piece()
        for _ in b_tail():
            piece = next(pending, None)
            if piece is not None:
                piece()
        for piece in pending:
            piece()

        if chained:
            for h in range(N_HEADS):
                st_s[h] = state[h]
                so_ref[0, h] = state[h].T

    @pl.when(step % 2 == 0)
    def _():
        both_stages(0, 1)

    @pl.when(step % 2 == 1)
    def _():
        both_stages(1, 0)


def _mixer(x, seq_len, s0, g, w_in, lb_logits, out_gain, ln_g, ln_b, w_s, bias, w_out, layer):
    t = x.shape[0]
    n_seq = t // seq_len
    tm = min(TOKEN_TILE, t)
    assert t % tm == 0 and (tm // BLOCK) % GROUP_BLOCKS == 0
    n_tiles = t // tm
    chained = s0 is None
    if chained:
        assert seq_len % tm == 0 and tm % SGU_CHUNK == 0
        tiles_per_seq, seq_per_tile, sgu_c = seq_len // tm, 1, SGU_CHUNK
    else:
        assert seq_len == BLOCK and tm % BLOCK == 0
        tiles_per_seq, seq_per_tile, sgu_c = 1, tm // BLOCK, BLOCK
    lconst3 = lambda i: (layer, 0, 0)
    tile_a = lambda i: jnp.minimum(i, n_tiles - 1)
    tile_b = lambda i: jnp.maximum(i - 1, 0)
    state_spec = pl.BlockSpec((seq_per_tile, N_HEADS, HEAD, HEAD),
                              lambda i: (tile_b(i) // tiles_per_seq, 0, 0, 0))
    in_specs = [
        pl.BlockSpec((tm, D_MODEL), lambda i: (tile_a(i), 0)),
        pl.BlockSpec((tm, D_MODEL), lambda i: (tile_b(i), 0)),
        pl.BlockSpec((None, 1, D_MODEL), lconst3),
        pl.BlockSpec((None, D_MODEL, D_IN), lconst3),
        pl.BlockSpec(lb_logits.shape, lambda i: (0, 0)),
        pl.BlockSpec((None, 1, D_HGRN), lconst3),
        pl.BlockSpec((None, 1, D_SGU), lconst3),
        pl.BlockSpec((None, 1, D_SGU), lconst3),
        pl.BlockSpec((None, N_HEADS, SGU_CHUNK, SGU_CHUNK), lambda i: (layer, 0, 0, 0)),
        pl.BlockSpec((None, SGU_CHUNK, D_SGU), lconst3),
        pl.BlockSpec((None, D_MODEL, D_MODEL), lconst3),
    ]
    args = [x, x, g, w_in, lb_logits, out_gain, ln_g, ln_b, w_s, bias, w_out]
    out_specs = [pl.BlockSpec((tm, D_MODEL), lambda i: (tile_b(i), 0)), state_spec]
    out_shape = [jax.ShapeDtypeStruct((t, D_MODEL), F32),
                 jax.ShapeDtypeStruct((n_seq, N_HEADS, HEAD, HEAD), F32)]
    if not chained:
        in_specs.append(state_spec)
        args.append(s0)
        out_specs.append(pl.BlockSpec((tm, D_SGU), lambda i: (tile_a(i), 0)))
        out_shape.append(jax.ShapeDtypeStruct((t, D_SGU), F32))
    wide = lambda dt: pltpu.VMEM((2, tm, D_HGRN), dt)
    scratch = [wide(F32), wide(F32), wide(F32), wide(F32), wide(F32), wide(F32), wide(BF16),
               pltpu.VMEM((tm, D_MODEL), BF16), pltpu.VMEM((N_HEADS, HEAD, HEAD), F32)]
    return pl.pallas_call(
        functools.partial(_mixer_kernel, layer=layer, tm=tm, n_tiles=n_tiles, chained=chained,
                          tiles_per_seq=tiles_per_seq, sgu_c=sgu_c),
        grid=(n_tiles + 1,),
        in_specs=in_specs,
        out_specs=out_specs,
        out_shape=out_shape,
        scratch_shapes=scratch,
        compiler_params=pltpu.CompilerParams(
            dimension_semantics=("arbitrary",), vmem_limit_bytes=VMEM_LIMIT_BYTES),
        name="mixer_chained" if chained else "mixer_stepped",
    )(*args)


def _trunk(x3, s0_stack, p):
    bsz, seq_len, _ = x3.shape
    x = x3.reshape(bsz * seq_len, D_MODEL)
    depth = p["w_in"].shape[0]
    states, vrows = [], []
    for l in range(depth):
        x = _ffn(x, p["norm_ffn1"], p["ffn1_w1"], p["ffn1_w3"], p["ffn1_w2"], l)
        s0 = None if s0_stack is None else s0_stack[l]
        res = _mixer(x, seq_len, s0, p["norm_mix"], p["w_in"], p["lb_logits"], p["hgrn_out_norm"],
                     p["sgu_ln_g"], p["sgu_ln_b"], p["sgu_w_s"], p["sgu_bias"], p["w_out"], l)
        x = res[0]
        states.append(res[1])
        if s0 is not None:
            vrows.append(res[2].reshape(bsz, seq_len, D_SGU))
        x = _ffn(x, p["norm_ffn2"], p["ffn2_w1"], p["ffn2_w3"], p["ffn2_w2"], l,
                 final_g=p["final_norm"] if l == depth - 1 else None)
    y = x.reshape(bsz, seq_len, D_MODEL)
    return y, jnp.stack(states), (jnp.stack(vrows) if vrows else None)


def kernel(x_prompt, x_sample, state_hgrn, lb_logits, norm_ffn1, ffn1_w1, ffn1_w3, ffn1_w2, norm_mix, w_in, hgrn_out_norm, sgu_ln_g, sgu_ln_b, sgu_w_s, sgu_b_s, w_out, norm_ffn2, ffn2_w1, ffn2_w3, ffn2_w2, final_norm):
    row = lambda a: a.reshape(a.shape[0], 1, a.shape[1])
    p = dict(
        lb_logits=lb_logits,
        norm_ffn1=row(norm_ffn1), norm_mix=row(norm_mix), norm_ffn2=row(norm_ffn2),
        hgrn_out_norm=row(hgrn_out_norm), sgu_ln_g=row(sgu_ln_g), sgu_ln_b=row(sgu_ln_b),
        final_norm=final_norm.reshape(1, D_MODEL),
        ffn1_w1=ffn1_w1.astype(BF16), ffn1_w3=ffn1_w3.astype(BF16), ffn1_w2=ffn1_w2.astype(BF16),
        ffn2_w1=ffn2_w1.astype(BF16), ffn2_w3=ffn2_w3.astype(BF16), ffn2_w2=ffn2_w2.astype(BF16),
        w_in=w_in.astype(BF16), w_out=w_out.astype(BF16),
        sgu_w_s=sgu_w_s,
        sgu_bias=jnp.repeat(jnp.swapaxes(sgu_b_s, 1, 2), HEAD, axis=2),
    )
    y_prompt, s_prompt, _ = _trunk(x_prompt, None, p)
    y_sample, s_sample, v_sample = _trunk(x_sample, state_hgrn, p)
    return (y_prompt, y_sample, s_prompt, s_sample, v_sample)
```

```python
import functools

import jax
import jax.numpy as jnp
from jax import lax
from jax.experimental import pallas as pl
from jax.experimental.pallas import tpu as pltpu

F32 = jnp.float32
BF16 = jnp.bfloat16

D_MODEL = 1024
D_FF = 2816
D_HGRN = 512
D_SGU = 512
N_HEADS = 4
HEAD = 128
D_IN = 4 * D_HGRN + 2 * D_SGU
BLOCK = 64
SUBLANES = 8
SUB = 4
GROUP_BLOCKS = 2
SGU_CHUNK = 128
EPS = 1e-6
FFN_HALF = 0.5

TOKEN_TILE = 512
FFN_PARTS = 2
FF_CHUNK = 256
A_COLS = 256
VMEM_LIMIT_BYTES = 56 * 1024 * 1024


def _rms(x, g):
    return x * lax.rsqrt(jnp.mean(x * x, axis=-1, keepdims=True) + EPS) * g


def _dot(a, b):
    return jnp.dot(a, b, preferred_element_type=F32)


def _dot_nt(a, b):
    return lax.dot_general(a, b, (((1,), (1,)), ((), ())), preferred_element_type=F32)


def _dot_tn(a, b):
    return lax.dot_general(a, b, (((0,), (0,)), ((), ())), preferred_element_type=F32)


def _ffn_kernel(*refs, final):
    if final:
        x_ref, g_ref, w1_ref, w3_ref, w2_ref, fg_ref, o_ref, h_s = refs
    else:
        x_ref, g_ref, w1_ref, w3_ref, w2_ref, o_ref, h_s = refs
    part_rows = x_ref.shape[0] // FFN_PARTS
    for part in range(FFN_PARTS):
        rows = slice(part * part_rows, (part + 1) * part_rows)
        x = x_ref[rows, :]
        xn = _rms(x, g_ref[...]).astype(BF16)
        for c in range(D_FF // FF_CHUNK):
            cs = slice(c * FF_CHUNK, (c + 1) * FF_CHUNK)
            a = _dot(xn, w1_ref[:, cs])
            b = _dot(xn, w3_ref[:, cs])
            h_s[rows, cs] = (jax.nn.silu(a) * b).astype(BF16)
        out = x + FFN_HALF * _dot(h_s[rows, :], w2_ref[...])
        if final:
            out = _rms(out, fg_ref[...])
        o_ref[rows, :] = out


def _ffn(x, g, w1, w3, w2, layer, final_g=None):
    t = x.shape[0]
    tm = min(FFN_PARTS * TOKEN_TILE, t)
    assert t % tm == 0 and tm % FFN_PARTS == 0
    final = final_g is not None
    const = lambda i: (layer, 0, 0)
    resident = dict(pipeline_mode=pl.Buffered(1))
    in_specs = [
        pl.BlockSpec((tm, D_MODEL), lambda i: (i, 0)),
        pl.BlockSpec((None, 1, D_MODEL), const),
        pl.BlockSpec((None, D_MODEL, D_FF), const, **resident),
        pl.BlockSpec((None, D_MODEL, D_FF), const, **resident),
        pl.BlockSpec((None, D_FF, D_MODEL), const, **resident),
    ]
    args = [x, g, w1, w3, w2]
    if final:
        in_specs.append(pl.BlockSpec((1, D_MODEL), lambda i: (0, 0)))
        args.append(final_g)
    return pl.pallas_call(
        functools.partial(_ffn_kernel, final=final),
        grid=(t // tm,),
        in_specs=in_specs,
        out_specs=pl.BlockSpec((tm, D_MODEL), lambda i: (i, 0)),
        out_shape=jax.ShapeDtypeStruct((t, D_MODEL), F32),
        scratch_shapes=[pltpu.VMEM((tm, D_FF), BF16)],
        compiler_params=pltpu.CompilerParams(
            dimension_semantics=("arbitrary",), vmem_limit_bytes=VMEM_LIMIT_BYTES),
        name="ffn_final" if final else "ffn",
    )(*args)


def _roll_rows(x, d):
    n, c = x.shape
    return pltpu.roll(x.reshape(n // SUBLANES, SUBLANES, c), d, 1).reshape(n, c)


def _level_operand(q, k, b2, half):
    if half >= SUBLANES:
        pieces = []
        for r0 in range(0, BLOCK, half):
            mid = (r0 // (2 * half)) * 2 * half + half - 1
            if (r0 // half) % 2 == 1:
                pieces.append(q[r0:r0 + half] * jnp.exp2(b2[r0:r0 + half] - b2[mid:mid + 1]))
            else:
                pieces.append(k[r0:r0 + half] * jnp.exp2(b2[mid:mid + 1] - b2[r0:r0 + half]))
        return jnp.concatenate(pieces, axis=0).astype(BF16)
    assert 2 * half == SUBLANES
    b3 = b2.reshape(BLOCK // SUBLANES, SUBLANES, HEAD)
    ref = jnp.broadcast_to(b3[:, half - 1:half, :], b3.shape).reshape(BLOCK, HEAD)
    upper = lax.broadcasted_iota(jnp.int32, (BLOCK, HEAD), 0) % SUBLANES >= half
    return (jnp.where(upper, q, k) * jnp.exp2(-jnp.abs(b2 - ref))).astype(BF16)


def _hgrn_group(load, state_in, finish, diag_masks, level_masks):
    qs, ks, vs, b2s = load()
    n = len(qs)
    chains = range(n)
    vbs = [v.astype(BF16) for v in vs]
    near = [[jnp.sum(qs[g] * ks[g], axis=-1, keepdims=True) for g in chains]]
    for d in range(1, SUB):
        es = [jnp.exp2(jnp.minimum(b2s[g] - _roll_rows(b2s[g], d), 0.0)) for g in chains]
        near.append([jnp.sum(qs[g] * _roll_rows(ks[g], d) * es[g], axis=-1, keepdims=True)
                     for g in chains])
    yield
    far = []
    for half, _ in level_masks:
        ws = [_level_operand(qs[g], ks[g], b2s[g], half) for g in chains]
        far.append([_dot_nt(w, w) for w in ws])
    yield
    b_last = [b2[BLOCK - 1:BLOCK] for b2 in b2s]
    kdec = [(ks[g] * jnp.exp2(b_last[g] - b2s[g])).astype(BF16) for g in chains]
    grow = [_dot_tn(vbs[g], kdec[g]) for g in chains]
    st_in = [state_in(g, lambda st, g=g: st * jnp.exp2(b_last[g]) + grow[g]) for g in chains]
    qdec = [(qs[g] * jnp.exp2(b2s[g])).astype(BF16) for g in chains]
    o = [_dot_nt(qdec[g], st_in[g].astype(BF16)) for g in chains]
    yield
    att = [jnp.where(diag_masks[0], near[0][g], 0.0) for g in chains]
    for d in range(1, SUB):
        att = [jnp.where(diag_masks[d], near[d][g], att[g]) for g in chains]
    for lvl, (_, mask) in enumerate(level_masks):
        att = [jnp.where(mask, far[lvl][g], att[g]) for g in chains]
    finish([o[g] + _dot(att[g].astype(BF16), vbs[g]) for g in chains])
    yield


def _mixer_kernel(*refs, layer, tm, n_tiles, chained, tiles_per_seq, sgu_c):
    if chained:
        (xa_ref, xb_ref, g_ref, win_ref, lbl_ref, og_ref, lng_ref, lnb_ref, ws_ref, bs_ref, wout_ref,
         xo_ref, so_ref,
         q_s, k_s, v_s, lf_s, gate_s, u_s, vn_s, mix_s, st_s, hn_s) = refs
        s0_ref = vno_ref = None
    else:
        (xa_ref, xb_ref, g_ref, win_ref, lbl_ref, og_ref, lng_ref, lnb_ref, ws_ref, bs_ref, wout_ref,
         s0_ref,
         xo_ref, so_ref, vno_ref,
         q_s, k_s, v_s, lf_s, gate_s, u_s, vn_s, mix_s, st_s, hn_s) = refs
    step = pl.program_id(0)
    tile_b = jnp.maximum(step - 1, 0)

    @pl.when(step == 0)
    def _():
        for ref in (q_s, k_s, v_s, lf_s, gate_s, u_s, vn_s):
            ref[1] = jnp.zeros(ref.shape[1:], ref.dtype)
        hn_s[...] = _rms(xb_ref[...], g_ref[...]).astype(BF16)

    if chained:
        @pl.when(tile_b % tiles_per_seq == 0)
        def _():
            st_s[...] = jnp.zeros_like(st_s)

    def both_stages(slot_a, slot_b):
        state = [st_s[h] for h in range(N_HEADS)] if chained else None

        hn = hn_s[...]

        lg = lbl_ref[...]
        e = jnp.exp(lg - jnp.max(lg, axis=0, keepdims=True))
        p = e / jnp.sum(e, axis=0, keepdims=True)
        lb = p[0:1]
        for j in range(1, layer + 1):
            lb = lb + p[j:j + 1]
        lb = lb - p[0:1]

        def a_piece(group, half):
            cs = slice(half * A_COLS, (half + 1) * A_COLS)
            z = _dot(hn, win_ref[:, group * D_HGRN + half * A_COLS:group * D_HGRN + (half + 1) * A_COLS])
            if group == 0:
                q_s[slot_a, :, cs] = jax.nn.silu(z) * (HEAD ** -0.5)
            elif group == 1:
                f = lb[:, cs] + (1.0 - lb[:, cs]) * jax.nn.sigmoid(z)
                lf_s[slot_a, :, cs] = jnp.log2(f)
                k_s[slot_a, :, cs] = 1.0 - f
            elif group == 2:
                v_s[slot_a, :, cs] = z
            elif group == 3:
                gate_s[slot_a, :, cs] = jax.nn.silu(z)
            elif group == 4:
                u_s[slot_a, :, cs] = jax.nn.gelu(z)
            else:
                vg = jax.nn.gelu(z)
                for h in range(A_COLS // HEAD):
                    hs = slice(h * HEAD, (h + 1) * HEAD)
                    os_ = slice(half * A_COLS + h * HEAD, half * A_COLS + (h + 1) * HEAD)
                    vh = vg[:, hs]
                    mu = jnp.mean(vh, axis=-1, keepdims=True)
                    var = jnp.mean(jnp.square(vh - mu), axis=-1, keepdims=True)
                    vn = (vh - mu) * lax.rsqrt(var + EPS) * lng_ref[:, os_] + lnb_ref[:, os_]
                    if vno_ref is not None:
                        vno_ref[:, os_] = vn
                    vn_s[slot_a, :, os_] = vn.astype(BF16)

        a_pieces = [functools.partial(a_piece, group, half)
                    for group in range(D_IN // D_HGRN) for half in range(D_HGRN // A_COLS)]

        ti = lax.broadcasted_iota(jnp.int32, (BLOCK, BLOCK), 0)
        si = lax.broadcasted_iota(jnp.int32, (BLOCK, BLOCK), 1)
        tril3 = jnp.concatenate([(ti >= si).astype(BF16)] * 3, axis=1)
        diag_masks = [ti - si == d for d in range(SUB)]
        level_masks = []
        half = BLOCK // 2
        while half >= SUB:
            same = (ti // (2 * half)) == (si // (2 * half))
            level_masks.append((half, same & ((ti % (2 * half)) >= half) & ((si % (2 * half)) < half)))
            half //= 2

        def cum_log2_decay(i):
            lf = lf_s[slot_b, i * BLOCK:(i + 1) * BLOCK, :]
            lf_hi = lf.astype(BF16)
            rem = lf - lf_hi.astype(F32)
            lf_mid = rem.astype(BF16)
            lf_lo = (rem - lf_mid.astype(F32)).astype(BF16)
            return _dot(tril3, jnp.concatenate([lf_hi, lf_mid, lf_lo], axis=0))

        def b_hgrn(i0):
            blocks = range(i0, i0 + GROUP_BLOCKS)
            ids = [(i, h) for i in blocks for h in range(N_HEADS)]
            rows = [slice(i * BLOCK, (i + 1) * BLOCK) for i, _ in ids]
            cols = [slice(h * HEAD, (h + 1) * HEAD) for _, h in ids]

            def load():
                b2_all = {i: cum_log2_decay(i) for i in blocks}
                return ([q_s[slot_b, r, c] for r, c in zip(rows, cols)],
                        [k_s[slot_b, r, c] for r, c in zip(rows, cols)],
                        [v_s[slot_b, r, c] for r, c in zip(rows, cols)],
                        [b2_all[i][:, c] for (i, _), c in zip(ids, cols)])

            def state_in(g, update):
                i, h = ids[g]
                if chained:
                    st = state[h]
                    state[h] = update(st)
                else:
                    st = s0_ref[i, h].T
                    so_ref[i, h] = update(st).T
                return st

            def finish(outs):
                ms = [jnp.mean(o * o, axis=-1, keepdims=True) for o in outs]
                for g, (r, c) in enumerate(zip(rows, cols)):
                    o = outs[g] * lax.rsqrt(ms[g] + EPS) * og_ref[:, c]
                    mix_s[r, c] = (o * gate_s[slot_b, r, c]).astype(BF16)

            return _hgrn_group(load, state_in, finish, diag_masks, level_masks)

        def b_tail():
            tc = lax.broadcasted_iota(jnp.int32, (sgu_c, sgu_c), 0)
            sc = lax.broadcasted_iota(jnp.int32, (sgu_c, sgu_c), 1)
            for h in range(N_HEADS):
                cs = slice(h * HEAD, (h + 1) * HEAD)
                ws = jnp.where(tc >= sc, ws_ref[h, :sgu_c, :sgu_c], 0.0).astype(BF16)
                for j in range(tm // sgu_c):
                    rows = slice(j * sgu_c, (j + 1) * sgu_c)
                    mixed = _dot(ws, vn_s[slot_b, rows, cs]) + bs_ref[:sgu_c, cs]
                    mix_s[rows, D_HGRN + h * HEAD:D_HGRN + (h + 1) * HEAD] = (
                        u_s[slot_b, rows, cs] * mixed).astype(BF16)
                yield
            xo_ref[...] = xb_ref[...] + _dot(mix_s[...], wout_ref[...])
            yield

        pending = iter(a_pieces)
        segments = [b_hgrn(i0) for i0 in range(0, tm // BLOCK, GROUP_BLOCKS)] + [b_tail()]
        for segment in segments:
            for _ in segment:
                piece = next(pending, None)
                if piece is not None:
                    piece()
        for piece in pending:
            piece()

        if chained:
            for h in range(N_HEADS):
                st_s[h] = state[h]
                so_ref[0, h] = state[h].T
        hn_s[...] = _rms(xa_ref[...], g_ref[...]).astype(BF16)

    @pl.when(step % 2 == 0)
    def _():
        both_stages(0, 1)

    @pl.when(step % 2 == 1)
    def _():
        both_stages(1, 0)


def _mixer(x, seq_len, s0, g, w_in, lb_logits, out_gain, ln_g, ln_b, w_s, bias, w_out, layer):
    t = x.shape[0]
    n_seq = t // seq_len
    tm = min(TOKEN_TILE, t)
    assert t % tm == 0 and (tm // BLOCK) % GROUP_BLOCKS == 0
    n_tiles = t // tm
    chained = s0 is None
    if chained:
        assert seq_len % tm == 0 and tm % SGU_CHUNK == 0
        tiles_per_seq, seq_per_tile, sgu_c = seq_len // tm, 1, SGU_CHUNK
    else:
        assert seq_len == BLOCK and tm % BLOCK == 0
        tiles_per_seq, seq_per_tile, sgu_c = 1, tm // BLOCK, BLOCK
    lconst3 = lambda i: (layer, 0, 0)
    tile_a = lambda i: jnp.minimum(i, n_tiles - 1)
    tile_b = lambda i: jnp.maximum(i - 1, 0)
    state_spec = pl.BlockSpec((seq_per_tile, N_HEADS, HEAD, HEAD),
                              lambda i: (tile_b(i) // tiles_per_seq, 0, 0, 0))
    in_specs = [
        pl.BlockSpec((tm, D_MODEL), lambda i: (tile_a(i + 1), 0)),
        pl.BlockSpec((tm, D_MODEL), lambda i: (tile_b(i), 0)),
        pl.BlockSpec((None, 1, D_MODEL), lconst3),
        pl.BlockSpec((None, D_MODEL, D_IN), lconst3),
        pl.BlockSpec(lb_logits.shape, lambda i: (0, 0)),
        pl.BlockSpec((None, 1, D_HGRN), lconst3),
        pl.BlockSpec((None, 1, D_SGU), lconst3),
        pl.BlockSpec((None, 1, D_SGU), lconst3),
        pl.BlockSpec((None, N_HEADS, SGU_CHUNK, SGU_CHUNK), lambda i: (layer, 0, 0, 0)),
        pl.BlockSpec((None, SGU_CHUNK, D_SGU), lconst3),
        pl.BlockSpec((None, D_MODEL, D_MODEL), lconst3),
    ]
    args = [x, x, g, w_in, lb_logits, out_gain, ln_g, ln_b, w_s, bias, w_out]
    out_specs = [pl.BlockSpec((tm, D_MODEL), lambda i: (tile_b(i), 0)), state_spec]
    out_shape = [jax.ShapeDtypeStruct((t, D_MODEL), F32),
                 jax.ShapeDtypeStruct((n_seq, N_HEADS, HEAD, HEAD), F32)]
    if not chained:
        in_specs.append(state_spec)
        args.append(s0)
        out_specs.append(pl.BlockSpec((tm, D_SGU), lambda i: (tile_a(i), 0)))
        out_shape.append(jax.ShapeDtypeStruct((t, D_SGU), F32))
    wide = lambda dt: pltpu.VMEM((2, tm, D_HGRN), dt)
    scratch = [wide(F32), wide(F32), wide(F32), wide(F32), wide(F32), wide(F32), wide(BF16),
               pltpu.VMEM((tm, D_MODEL), BF16), pltpu.VMEM((N_HEADS, HEAD, HEAD), F32),
               pltpu.VMEM((tm, D_MODEL), BF16)]
    return pl.pallas_call(
        functools.partial(_mixer_kernel, layer=layer, tm=tm, n_tiles=n_tiles, chained=chained,
                          tiles_per_seq=tiles_per_seq, sgu_c=sgu_c),
        grid=(n_tiles + 1,),
        in_specs=in_specs,
        out_specs=out_specs,
        out_shape=out_shape,
        scratch_shapes=scratch,
        compiler_params=pltpu.CompilerParams(
            dimension_semantics=("arbitrary",), vmem_limit_bytes=VMEM_LIMIT_BYTES),
        name="mixer_chained" if chained else "mixer_stepped",
    )(*args)


def _trunk(x3, s0_stack, p):
    bsz, seq_len, _ = x3.shape
    x = x3.reshape(bsz * seq_len, D_MODEL)
    depth = p["w_in"].shape[0]
    states, vrows = [], []
    for l in range(depth):
        x = _ffn(x, p["norm_ffn1"], p["ffn1_w1"], p["ffn1_w3"], p["ffn1_w2"], l)
        s0 = None if s0_stack is None else s0_stack[l]
        res = _mixer(x, seq_len, s0, p["norm_mix"], p["w_in"], p["lb_logits"], p["hgrn_out_norm"],
                     p["sgu_ln_g"], p["sgu_ln_b"], p["sgu_w_s"], p["sgu_bias"], p["w_out"], l)
        x = res[0]
        states.append(res[1])
        if s0 is not None:
            vrows.append(res[2].reshape(bsz, seq_len, D_SGU))
        x = _ffn(x, p["norm_ffn2"], p["ffn2_w1"], p["ffn2_w3"], p["ffn2_w2"], l,
                 final_g=p["final_norm"] if l == depth - 1 else None)
    y = x.reshape(bsz, seq_len, D_MODEL)
    return y, jnp.stack(states), (jnp.stack(vrows) if vrows else None)


def kernel(x_prompt, x_sample, state_hgrn, lb_logits, norm_ffn1, ffn1_w1, ffn1_w3, ffn1_w2, norm_mix, w_in, hgrn_out_norm, sgu_ln_g, sgu_ln_b, sgu_w_s, sgu_b_s, w_out, norm_ffn2, ffn2_w1, ffn2_w3, ffn2_w2, final_norm):
    row = lambda a: a.reshape(a.shape[0], 1, a.shape[1])
    p = dict(
        lb_logits=lb_logits,
        norm_ffn1=row(norm_ffn1), norm_mix=row(norm_mix), norm_ffn2=row(norm_ffn2),
        hgrn_out_norm=row(hgrn_out_norm), sgu_ln_g=row(sgu_ln_g), sgu_ln_b=row(sgu_ln_b),
        final_norm=final_norm.reshape(1, D_MODEL),
        ffn1_w1=ffn1_w1.astype(BF16), ffn1_w3=ffn1_w3.astype(BF16), ffn1_w2=ffn1_w2.astype(BF16),
        ffn2_w1=ffn2_w1.astype(BF16), ffn2_w3=ffn2_w3.astype(BF16), ffn2_w2=ffn2_w2.astype(BF16),
        w_in=w_in.astype(BF16), w_out=w_out.astype(BF16),
        sgu_w_s=sgu_w_s,
        sgu_bias=jnp.repeat(jnp.swapaxes(sgu_b_s, 1, 2), HEAD, axis=2),
    )
    y_prompt, s_prompt, _ = _trunk(x_prompt, None, p)
    y_sample, s_sample, v_sample = _trunk(x_sample, state_hgrn, p)
    return (y_prompt, y_sample, s_prompt, s_sample, v_sample)
```
